```python
import jax, jax.numpy as jnp
from jax import lax
import numpy as np

D_MODEL = 1024
BATCH = 1
SEQ = 16384
DEPTH = 2
DEC_BATCH = 32
DEC_SEQ = 1
PAST_LEN = 16384
PAGE_SIZE = 128

MIX_WIDTH = D_MODEL
HEAD_DIM = 64
A_WIDTH = MIX_WIDTH // 4
A_GROUPS = A_WIDTH // HEAD_DIM
B_WIDTH = MIX_WIDTH - A_WIDTH
B_HEADS = B_WIDTH // HEAD_DIM
CHUNK = 128
PATTERNS = ((128, 1), (512, 4), (2048, 16))
WINDOW = 2048
Q_BLOCK = 128
_SIZES = (A_WIDTH, A_WIDTH, A_WIDTH, B_WIDTH, B_WIDTH, B_WIDTH, B_WIDTH)
PROJ_WIDTH = sum(_SIZES)
SPLIT_AT = tuple(sum(_SIZES[:i + 1]) for i in range(len(_SIZES) - 1))
EPS = 1e-6

kernel_name = "hymba_sgu_dilated_swa_decoder_step"


def rms_norm(x, g):
    xf = x.astype(jnp.float32)
    y = xf * lax.rsqrt(jnp.mean(xf * xf, axis=-1, keepdims=True) + EPS)
    return (y * g.astype(jnp.float32)).astype(x.dtype)


def alibi_slopes():
    return 2.0 ** (-8.0 * jnp.arange(1, B_HEADS + 1, dtype=jnp.float32) / B_HEADS)


def chunk_spatial_gate(u, v, sgu_g, w_s, b_s):
    bn, t, _ = v.shape
    n_chunks = -(-t // CHUNK)
    pad = n_chunks * CHUNK - t
    vn = rms_norm(v, sgu_g)
    vp = jnp.pad(vn, ((0, 0), (0, pad), (0, 0))).reshape(bn, n_chunks, CHUNK, A_GROUPS, HEAD_DIM)
    mask = jnp.tril(jnp.ones((CHUNK, CHUNK), dtype=bool))
    w = jnp.where(mask[None], w_s, jnp.zeros_like(w_s))
    mixed = jnp.einsum('gts,bcsgd->bctgd', w, vp) + b_s.T[None, None, :, :, None]
    mixed = mixed.reshape(bn, n_chunks * CHUNK, A_WIDTH)[:, :t]
    return u * mixed, vn


def dilated_attend(q, k, v, q_idx, slopes):
    outs, lses = [], []
    for win, dil in PATTERNS:
        j = jnp.arange(win // dil + 1)
        idx = q_idx[:, None] - j[None, :] * dil
        valid = idx >= 0
        idxc = jnp.maximum(idx, 0)
        kg = k[:, idxc]
        vg = v[:, idxc]
        s = jnp.einsum('bqhd,bqjhd->bqhj', q, kg, preferred_element_type=jnp.float32)
        dist = (j * dil).astype(jnp.float32)
        s = s - slopes[:, None] * dist[None, :]
        s = jnp.where(valid[None, :, None, :], s, -jnp.inf)
        m = jnp.max(s, axis=-1, keepdims=True)
        p = jnp.exp(s - m)
        den = jnp.sum(p, axis=-1, keepdims=True)
        o = jnp.einsum('bqhj,bqjhd->bqhd', (p / den).astype(v.dtype), vg,
                       preferred_element_type=jnp.float32)
        outs.append(o)
        lses.append((m + jnp.log(den))[..., 0])
    wts = jax.nn.softmax(jnp.stack(lses, axis=0), axis=0)
    out = jnp.einsum('pbqh,pbqhd->bqhd', wts, jnp.stack(outs, axis=0))
    return out.astype(q.dtype)


def mixer_inputs(x, norm_g, w_in, q_g, k_g):
    bn, t, _ = x.shape
    h = rms_norm(x, norm_g)
    p = jnp.einsum('btd,de->bte', h, w_in)
    ua, va, za, q, k, v, zb = jnp.split(p, SPLIT_AT, axis=-1)
    q = rms_norm(q.reshape(bn, t, B_HEADS, HEAD_DIM), q_g) * (HEAD_DIM ** -0.5)
    k = rms_norm(k.reshape(bn, t, B_HEADS, HEAD_DIM), k_g)
    v = v.reshape(bn, t, B_HEADS, HEAD_DIM)
    return ua, va, za, q, k, v, zb


def mixer_output(x, a, za, o, zb, w_out):
    bn, t, _ = x.shape
    a = a * jax.nn.silu(za)
    b = o.reshape(bn, t, B_WIDTH) * jax.nn.silu(zb)
    return x + jnp.einsum('bte,ed->btd', jnp.concatenate([a, b], axis=-1), w_out)


def setup_inputs(seed: int = 0) -> dict:
    key = jax.random.key(seed)
    ks = jax.random.split(key, 12)
    cache_len = min(WINDOW, PAST_LEN)
    f32 = jnp.float32
    return {
        "x_prompt": jax.random.normal(ks[0], (BATCH, SEQ, D_MODEL), f32),
        "x_sample": jax.random.normal(ks[1], (DEC_BATCH, DEC_SEQ, D_MODEL), f32),
        "cache_k": jax.random.normal(ks[2], (DEPTH, DEC_BATCH, cache_len, B_HEADS, HEAD_DIM), f32),
        "cache_v": jax.random.normal(ks[3], (DEPTH, DEC_BATCH, cache_len, B_HEADS, HEAD_DIM), f32),
        "norm_g": 1.0 + 0.02 * jax.random.normal(ks[4], (DEPTH, D_MODEL), f32),
        "w_in": jax.random.normal(ks[5], (DEPTH, D_MODEL, PROJ_WIDTH), f32) * D_MODEL ** -0.5,
        "sgu_g": 1.0 + 0.02 * jax.random.normal(ks[6], (DEPTH, A_WIDTH), f32),
        "w_spatial": jax.random.normal(ks[7], (DEPTH, A_GROUPS, CHUNK, CHUNK), f32) * CHUNK ** -0.5,
        "b_spatial": 0.1 * jax.random.normal(ks[8], (DEPTH, A_GROUPS, CHUNK), f32),
        "q_norm_g": 1.0 + 0.02 * jax.random.normal(ks[9], (DEPTH, HEAD_DIM), f32),
        "k_norm_g": 1.0 + 0.02 * jax.random.normal(ks[10], (DEPTH, HEAD_DIM), f32),
        "w_out": jax.random.normal(ks[11], (DEPTH, MIX_WIDTH, D_MODEL), f32) * MIX_WIDTH ** -0.5,
    }


def reference(x_prompt, x_sample, cache_k, cache_v, norm_g, w_in, sgu_g, w_spatial, b_spatial,
              q_norm_g, k_norm_g, w_out):
    slopes = alibi_slopes()
    xp, xs = x_prompt, x_sample
    bp, tp, _ = xp.shape
    bs, ts, _ = xs.shape
    n_blocks = tp // Q_BLOCK
    keep_p = min(WINDOW, tp)
    cache_len = cache_k.shape[2]
    kp_new, vp_new, ks_new, vs_new, sgu_new = [], [], [], [], []
    for l in range(DEPTH):
        ua, va, za, q, k, v, zb = mixer_inputs(xp, norm_g[l], w_in[l], q_norm_g[l], k_norm_g[l])
        a_out, _ = chunk_spatial_gate(ua, va, sgu_g[l], w_spatial[l], b_spatial[l])
        qb = q.reshape(bp, n_blocks, Q_BLOCK, B_HEADS, HEAD_DIM).transpose(1, 0, 2, 3, 4)

        def attend_block(args, k=k, v=v):
            q_blk, i = args
            q_idx = i * Q_BLOCK + jnp.arange(Q_BLOCK)
            return dilated_attend(q_blk, k, v, q_idx, slopes)

        ob = lax.map(attend_block, (qb, jnp.arange(n_blocks)))
        o = ob.transpose(1, 0, 2, 3, 4).reshape(bp, tp, B_HEADS, HEAD_DIM)
        kp_new.append(k[:, tp - keep_p:])
        vp_new.append(v[:, tp - keep_p:])
        xp = mixer_output(xp, a_out, za, o, zb, w_out[l])

        ua, va, za, q, k, v, zb = mixer_inputs(xs, norm_g[l], w_in[l], q_norm_g[l], k_norm_g[l])
        a_out, vn = chunk_spatial_gate(ua, va, sgu_g[l], w_spatial[l], b_spatial[l])
        k_ext = jnp.concatenate([cache_k[l].astype(k.dtype), k], axis=1)
        v_ext = jnp.concatenate([cache_v[l].astype(v.dtype), v], axis=1)
        o = dilated_attend(q, k_ext, v_ext, cache_len + jnp.arange(ts), slopes)
        ks_new.append(k)
        vs_new.append(v)
        sgu_new.append(vn)
        xs = mixer_output(xs, a_out, za, o, zb, w_out[l])

    new_k_prompt = jnp.stack(kp_new, axis=0)
    new_v_prompt = jnp.stack(vp_new, axis=0)
    new_k_sample = jnp.stack(ks_new, axis=0)
    new_v_sample = jnp.stack(vs_new, axis=0)
    new_sgu_v_sample = jnp.stack(sgu_new, axis=0)
    return (xp, xs, new_k_prompt, new_v_prompt, new_k_sample, new_v_sample, new_sgu_v_sample)
```

```python
import functools

import jax
import jax.numpy as jnp
from jax import lax
from jax.experimental import pallas as pl
from jax.experimental.pallas import tpu as pltpu

D_MODEL = 1024
HEAD_DIM = 64
A_WIDTH = 256
A_GROUPS = 4
B_WIDTH = 768
B_HEADS = 12
CHUNK = 128
PATTERNS = ((128, 1), (512, 4), (2048, 16))
WINDOW = 2048
PROJ_WIDTH = 3 * A_WIDTH + 4 * B_WIDTH
EPS = 1e-6

LANES = 128
BAND = 128
QKV_WIDTH = 3 * B_WIDTH
QKV_SLABS = QKV_WIDTH // LANES
O_SLABS = B_WIDTH // LANES
ACC_SLABS = O_SLABS + 1
STAT_L_OFFSET = 16
NEG_BIG = -1e30
PROJ_TILE = 256
OUT_TILE = 512
VMEM_LIMIT = 52 * 1024 * 1024

_UA, _VA, _ZA = 0, A_WIDTH, 2 * A_WIDTH
_Q = 3 * A_WIDTH
_K = _Q + B_WIDTH
_V = _K + B_WIDTH
_ZB = _V + B_WIDTH


def _alibi_slopes():
    return 2.0 ** (-8.0 * jnp.arange(1, B_HEADS + 1, dtype=jnp.float32) / B_HEADS)


def _silu(z):
    return z * (1.0 / (1.0 + jnp.exp(-z)))


def _rms_scale(x):
    return lax.rsqrt(jnp.mean(x * x, axis=-1, keepdims=True) + EPS)


def _head_mean_square(t, bd):
    sq = (t * t).astype(jnp.bfloat16)
    parts = [jnp.dot(sq[:, j * 256:(j + 1) * 256], bd, preferred_element_type=jnp.float32)
             for j in range(B_WIDTH // 256)]
    return jnp.concatenate(parts, axis=1) * (1.0 / HEAD_DIM)


def _project(x, ng, w_ref, sg, qg, kg, bd):
    h = (x * _rms_scale(x) * ng).astype(jnp.bfloat16)

    def proj(lo, width):
        return jnp.dot(h, w_ref[:, lo:lo + width], preferred_element_type=jnp.float32)

    ua = proj(_UA, A_WIDTH)
    va = proj(_VA, A_WIDTH)
    za = proj(_ZA, A_WIDTH)
    q = proj(_Q, B_WIDTH)
    k = proj(_K, B_WIDTH)
    v = proj(_V, B_WIDTH)
    zb = proj(_ZB, B_WIDTH)
    vn = va * _rms_scale(va) * sg
    qn = q * lax.rsqrt(_head_mean_square(q, bd) + EPS) * qg * (HEAD_DIM ** -0.5)
    kn = k * lax.rsqrt(_head_mean_square(k, bd) + EPS) * kg
    return ua, vn, za, qn, kn, v, zb


def _prompt_proj_kernel(x_ref, ng_ref, w_ref, sg_ref, wsp_ref, bsp_ref, qg_ref, kg_ref, bd_ref,
                        nat_ref, p4_ref, p16_ref, kf_ref, vf_ref, zb_ref, ag_ref,
                        wtril_scr, de_scr):
    @pl.when(pl.program_id(0) == 0)
    def _():
        t = lax.broadcasted_iota(jnp.int32, (CHUNK, A_GROUPS * CHUNK), 0)
        s = lax.broadcasted_iota(jnp.int32, (CHUNK, A_GROUPS * CHUNK), 1) % CHUNK
        wtril_scr[...] = jnp.where(s <= t, wsp_ref[...], 0.0).astype(jnp.bfloat16)

    ua, vn, za, qn, kn, v, zb = _project(x_ref[...], ng_ref[...], w_ref, sg_ref[...], qg_ref[...],
                                         kg_ref[...], bd_ref[...])

    group = lax.broadcasted_iota(jnp.int32, (CHUNK, A_WIDTH), 1) // HEAD_DIM
    vnb = vn.astype(jnp.bfloat16)
    mixed = []
    for c in range(PROJ_TILE // CHUNK):
        vc = vnb[c * CHUNK:(c + 1) * CHUNK]
        rhs = jnp.concatenate([jnp.where(group == g, vc, jnp.zeros_like(vc)) for g in range(A_GROUPS)], axis=0)
        mixed.append(jnp.dot(wtril_scr[...], rhs, preferred_element_type=jnp.float32) + bsp_ref[...])
    mixed = jnp.concatenate(mixed, axis=0)
    ag_ref[...] = (ua * mixed * _silu(za)).astype(ag_ref.dtype)

    kf_ref[...] = kn
    vf_ref[...] = v
    zb_ref[...] = zb
    qkv = jnp.concatenate([qn, kn, v], axis=1)
    nat_ref[...] = qkv.astype(nat_ref.dtype)
    for s in range(QKV_SLABS):
        de_scr[s] = qkv[:, s * LANES:(s + 1) * LANES]
    for d, ref in ((4, p4_ref), (16, p16_ref)):
        rows = PROJ_TILE // d
        for r in range(d):
            for s in range(QKV_SLABS):
                ref[r, :, s * LANES:(s + 1) * LANES] = de_scr[s, pl.ds(r, rows, stride=d), :].astype(ref.dtype)


def _prompt_proj(x, ng, w_in, sg, wsp, bsp, qg, kg, bd):
    t = x.shape[0]
    n = t // PROJ_TILE
    const = lambda shape: pl.BlockSpec(shape, lambda i: (0,) * len(shape))
    row = lambda width: pl.BlockSpec((PROJ_TILE, width), lambda i: (i, 0))
    bf = jnp.bfloat16
    return pl.pallas_call(
        _prompt_proj_kernel,
        grid=(n,),
        in_specs=[row(D_MODEL), const((1, D_MODEL)), const((D_MODEL, PROJ_WIDTH)), const((1, A_WIDTH)),
                  const((CHUNK, A_GROUPS * CHUNK)), const((CHUNK, A_WIDTH)), const((1, B_WIDTH)),
                  const((1, B_WIDTH)), const((256, 256))],
        out_specs=[row(QKV_WIDTH),
                   pl.BlockSpec((4, PROJ_TILE // 4, QKV_WIDTH), lambda i: (0, i, 0)),
                   pl.BlockSpec((16, PROJ_TILE // 16, QKV_WIDTH), lambda i: (0, i, 0)),
                   row(B_WIDTH), row(B_WIDTH), row(B_WIDTH), row(A_WIDTH)],
        out_shape=[jax.ShapeDtypeStruct((t, QKV_WIDTH), bf),
                   jax.ShapeDtypeStruct((4, t // 4, QKV_WIDTH), bf),
                   jax.ShapeDtypeStruct((16, t // 16, QKV_WIDTH), bf),
                   jax.ShapeDtypeStruct((t, B_WIDTH), jnp.float32),
                   jax.ShapeDtypeStruct((t, B_WIDTH), jnp.float32),
                   jax.ShapeDtypeStruct((t, B_WIDTH), jnp.float32),
                   jax.ShapeDtypeStruct((t, A_WIDTH), bf)],
        scratch_shapes=[pltpu.VMEM((CHUNK, A_GROUPS * CHUNK), bf),
                        pltpu.VMEM((QKV_SLABS, PROJ_TILE, LANES), jnp.float32)],
        compiler_params=pltpu.CompilerParams(dimension_semantics=("arbitrary",), vmem_limit_bytes=VMEM_LIMIT),
        name="prompt_proj",
    )(x, ng, w_in, sg, wsp, bsp, qg, kg, bd)


def _band_bias(d):
    a = jnp.arange(BAND)[:, None]
    c = jnp.arange(2 * BAND)[None, :]
    delta = a - c + BAND
    valid = (delta >= 0) & (delta <= BAND)
    dist = (delta * d).astype(jnp.float32)
    pen = -(_alibi_slopes()[:, None, None] * dist[None])
    return jnp.where(valid[None], pen, NEG_BIG)


def _attend_block(q, k, v, bias_ref, col0):
    kw = k.shape[0]
    low = lax.broadcasted_iota(jnp.int32, (1, LANES), 1) < HEAD_DIM
    lane = lax.broadcasted_iota(jnp.int32, (1, LANES), 1)
    stats = jnp.zeros((BAND, LANES), jnp.float32)
    tiles = []
    for hp in range(O_SLABS):
        qp = q[:, hp * LANES:(hp + 1) * LANES]
        kp = k[:, hp * LANES:(hp + 1) * LANES]
        vp = v[:, hp * LANES:(hp + 1) * LANES]
        halves = []
        for half in range(2):
            h = 2 * hp + half
            sel = low if half == 0 else jnp.logical_not(low)
            qh = jnp.where(sel, qp, jnp.zeros_like(qp))
            s = lax.dot_general(qh, kp, (((1,), (1,)), ((), ())), preferred_element_type=jnp.float32)
            s = s + bias_ref[h, :, col0:col0 + kw]
            m = jnp.max(s, axis=-1, keepdims=True)
            p = jnp.exp(s - m)
            l = jnp.sum(p, axis=-1, keepdims=True)
            halves.append(jnp.dot(p.astype(jnp.bfloat16), vp, preferred_element_type=jnp.float32))
            stats = jnp.where(lane == h, m, stats)
            stats = jnp.where(lane == STAT_L_OFFSET + h, l, stats)
        tiles.append(jnp.where(low, halves[0], halves[1]))
    tiles.append(stats)
    return tiles


def _attn_kernel(q_ref, kp_ref, kc_ref, vp_ref, vc_ref, bias_ref, out_ref, *, d):
    i = pl.program_id(0)
    r = pl.program_id(1)

    def store(tiles):
        for s, tile in enumerate(tiles):
            if d == 1:
                out_ref[s] = tile
            else:
                out_ref[s, pl.ds(r, BAND, stride=d), :] = tile

    @pl.when(i == 0)
    def _():
        store(_attend_block(q_ref[...], kc_ref[...], vc_ref[...], bias_ref, BAND))

    @pl.when(i > 0)
    def _():
        k = jnp.concatenate([kp_ref[...], kc_ref[...]], axis=0)
        v = jnp.concatenate([vp_ref[...], vc_ref[...]], axis=0)
        store(_attend_block(q_ref[...], k, v, bias_ref, 0))


def _attn_pass(qkv_d, d):
    t = qkv_d.shape[0] * qkv_d.shape[1]
    nblk = qkv_d.shape[1] // BAND
    blk = lambda col, prev: pl.BlockSpec(
        (None, BAND, B_WIDTH),
        (lambda i, r: (r, jnp.maximum(i - 1, 0), col)) if prev else (lambda i, r: (r, i, col)))
    return pl.pallas_call(
        functools.partial(_attn_kernel, d=d),
        grid=(nblk, d),
        in_specs=[blk(0, False), blk(1, True), blk(1, False), blk(2, True), blk(2, False),
                  pl.BlockSpec((B_HEADS, BAND, 2 * BAND), lambda i, r: (0, 0, 0))],
        out_specs=pl.BlockSpec((ACC_SLABS, d * BAND, LANES), lambda i, r: (0, i, 0)),
        out_shape=jax.ShapeDtypeStruct((ACC_SLABS, t, LANES), jnp.float32),
        compiler_params=pltpu.CompilerParams(dimension_semantics=("arbitrary", "arbitrary"),
                                             vmem_limit_bytes=VMEM_LIMIT),
        name=f"prompt_attn_d{d}",
    )(qkv_d, qkv_d, qkv_d, qkv_d, qkv_d, _band_bias(d))


def _expand_heads(scale, sel):
    out = None
    rem = scale
    for _ in range(3):
        piece = rem.astype(jnp.bfloat16)
        rem = rem - piece.astype(jnp.float32)
        term = jnp.dot(piece, sel, preferred_element_type=jnp.float32)
        out = term if out is None else out + term
    return out


def _merge_gate_project(accs, zb, ag, x, w_ref, sel):
    lane = lax.broadcasted_iota(jnp.int32, (1, LANES), 1)
    stats = [st for _, st in accs]
    m = functools.reduce(jnp.maximum, stats)
    ws = [jnp.exp(st - m) for st in stats]
    den = sum(w * pltpu.roll(st, LANES - STAT_L_OFFSET, 1) for w, st in zip(ws, stats))
    o = None
    for w, (tiles, _) in zip(ws, accs):
        scale = jnp.where(lane < B_HEADS, w / den, 0.0)
        term = _expand_heads(scale, sel) * tiles
        o = term if o is None else o + term
    b = (o * _silu(zb)).astype(jnp.bfloat16)
    y = x + jnp.dot(ag, w_ref[0:A_WIDTH, :], preferred_element_type=jnp.float32)
    return y + jnp.dot(b, w_ref[A_WIDTH:, :], preferred_element_type=jnp.float32)


def _prompt_out_kernel(a1_ref, a4_ref, a16_ref, zb_ref, ag_ref, x_ref, w_ref, sel_ref, y_ref):
    accs = []
    for ref in (a1_ref, a4_ref, a16_ref):
        tiles = jnp.concatenate([ref[s] for s in range(O_SLABS)], axis=1)
        accs.append((tiles, ref[O_SLABS]))
    y_ref[...] = _merge_gate_project(accs, zb_ref[...], ag_ref[...], x_ref[...], w_ref, sel_ref[...])


def _head_select():
    h = jnp.arange(LANES)[:, None]
    c = jnp.arange(B_WIDTH)[None, :] // HEAD_DIM
    return (h == c).astype(jnp.bfloat16)


def _prompt_out(a1, a4, a16, zb, ag, x, w_out):
    t = x.shape[0]
    acc = pl.BlockSpec((ACC_SLABS, OUT_TILE, LANES), lambda i: (0, i, 0))
    row = lambda width: pl.BlockSpec((OUT_TILE, width), lambda i: (i, 0))
    const = lambda shape: pl.BlockSpec(shape, lambda i: (0, 0))
    return pl.pallas_call(
        _prompt_out_kernel,
        grid=(t // OUT_TILE,),
        in_specs=[acc, acc, acc, row(B_WIDTH), row(A_WIDTH), row(D_MODEL), const((D_MODEL, D_MODEL)),
                  const((LANES, B_WIDTH))],
        out_specs=row(D_MODEL),
        out_shape=jax.ShapeDtypeStruct((t, D_MODEL), jnp.float32),
        compiler_params=pltpu.CompilerParams(dimension_semantics=("arbitrary",), vmem_limit_bytes=VMEM_LIMIT),
        name="prompt_out",
    )(a1, a4, a16, zb, ag, x, w_out, _head_select())


def _sample_proj_kernel(x_ref, ng_ref, w_ref, sg_ref, w0_ref, b0_ref, qg_ref, kg_ref, bd_ref,
                        q_ref, k_ref, v_ref, zb_ref, ag_ref, vn_ref):
    ua, vn, za, qn, kn, v, zb = _project(x_ref[...], ng_ref[...], w_ref, sg_ref[...], qg_ref[...],
                                         kg_ref[...], bd_ref[...])
    mixed = w0_ref[...] * vn + b0_ref[...]
    ag_ref[...] = (ua * mixed * _silu(za)).astype(ag_ref.dtype)
    q_ref[...] = qn
    k_ref[...] = kn
    v_ref[...] = v
    zb_ref[...] = zb
    vn_ref[...] = vn


def _sample_proj(x, ng, w_in, sg, w0, b0, qg, kg, bd):
    n = x.shape[0]
    f32 = jnp.float32
    wide = jax.ShapeDtypeStruct((n, B_WIDTH), f32)
    return pl.pallas_call(
        _sample_proj_kernel,
        out_shape=[wide, wide, wide, wide, jax.ShapeDtypeStruct((n, A_WIDTH), jnp.bfloat16),
                   jax.ShapeDtypeStruct((n, A_WIDTH), f32)],
        compiler_params=pltpu.CompilerParams(vmem_limit_bytes=VMEM_LIMIT),
        name="sample_proj",
    )(x, ng, w_in, sg, w0, b0, qg, kg, bd)


def _sample_attn_kernel(q_ref, kn_ref, vn_ref, k1_ref, k4_ref, k16_ref, v1_ref, v4_ref, v16_ref,
                        slope_ref, o_ref):
    q = q_ref[...]
    slopes = slope_ref[...]
    steps = (BAND - lax.broadcasted_iota(jnp.int32, (BAND, 1, 1), 0)).astype(jnp.float32)
    s_new = jnp.sum(kn_ref[...] * q, axis=-1, keepdims=True)
    scores = []
    for (_, d), k_ref in zip(PATTERNS, (k1_ref, k4_ref, k16_ref)):
        s = jnp.sum(k_ref[...] * q[None], axis=-1, keepdims=True)
        scores.append(s - slopes[None] * (steps * float(d)))
    m = s_new
    for s in scores:
        m = jnp.maximum(m, jnp.max(s, axis=0))
    p_new = float(len(PATTERNS)) * jnp.exp(s_new - m)
    den = p_new
    num = p_new * vn_ref[...]
    for s, v_ref in zip(scores, (v1_ref, v4_ref, v16_ref)):
        p = jnp.exp(s - m[None])
        den = den + jnp.sum(p, axis=0)
        num = num + jnp.sum(p * v_ref[...], axis=0)
    o_ref[...] = num / den


def _sample_attn(q, kn, vn, ck, cv, layer):
    n = q.shape[0]
    tok = pl.BlockSpec((None, B_HEADS, HEAD_DIM), lambda b: (b, 0, 0))
    cache_args, cache_specs = [], []
    for c in (ck, cv):
        for _, d in PATTERNS:
            rows = WINDOW // d
            cache_args.append(c.reshape(c.shape[0], n, rows, d, B_HEADS, HEAD_DIM))
            cache_specs.append(pl.BlockSpec((None, None, BAND, None, B_HEADS, HEAD_DIM),
                                            functools.partial(lambda b, blk: (layer, b, blk, 0, 0, 0),
                                                              blk=rows // BAND - 1)))
    return pl.pallas_call(
        _sample_attn_kernel,
        grid=(n,),
        in_specs=[tok, tok, tok] + cache_specs + [pl.BlockSpec((B_HEADS, 1), lambda b: (0, 0))],
        out_specs=tok,
        out_shape=jax.ShapeDtypeStruct((n, B_HEADS, HEAD_DIM), jnp.float32),
        compiler_params=pltpu.CompilerParams(dimension_semantics=("arbitrary",)),
        name="sample_attn",
    )(q, kn, vn, *cache_args, _alibi_slopes()[:, None])


def _sample_out_kernel(o_ref, zb_ref, ag_ref, x_ref, w_ref, y_ref):
    b = (o_ref[...] * _silu(zb_ref[...])).astype(jnp.bfloat16)
    y = x_ref[...] + jnp.dot(ag_ref[...], w_ref[0:A_WIDTH, :], preferred_element_type=jnp.float32)
    y_ref[...] = y + jnp.dot(b, w_ref[A_WIDTH:, :], preferred_element_type=jnp.float32)


def _sample_out(o, zb, ag, x, w_out):
    return pl.pallas_call(
        _sample_out_kernel,
        out_shape=jax.ShapeDtypeStruct(x.shape, jnp.float32),
        name="sample_out",
    )(o, zb, ag, x, w_out)


def kernel(x_prompt, x_sample, cache_k, cache_v, norm_g, w_in, sgu_g, w_spatial, b_spatial,
           q_norm_g, k_norm_g, w_out):
    depth = norm_g.shape[0]
    bp, tp, _ = x_prompt.shape
    bs, ts, _ = x_sample.shape
    assert bp == 1 and ts == 1 and cache_k.shape[2] == WINDOW and tp % (16 * BAND) == 0
    keep = min(WINDOW, tp)
    xp = x_prompt.reshape(tp, D_MODEL)
    xs = x_sample.reshape(bs, D_MODEL)
    bd = (jnp.arange(256)[:, None] // HEAD_DIM == jnp.arange(256)[None, :] // HEAD_DIM).astype(jnp.bfloat16)
    kp_new, vp_new, ks_new, vs_new, sgu_new = [], [], [], [], []
    for l in range(depth):
        ng = norm_g[l][None, :]
        sg = sgu_g[l][None, :]
        qg = jnp.tile(q_norm_g[l], B_HEADS)[None, :]
        kg = jnp.tile(k_norm_g[l], B_HEADS)[None, :]
        w_in_l = w_in[l].astype(jnp.bfloat16)
        w_out_l = w_out[l].astype(jnp.bfloat16)
        wsp = w_spatial[l].transpose(1, 0, 2).reshape(CHUNK, A_GROUPS * CHUNK)
        bsp = jnp.repeat(b_spatial[l].T, HEAD_DIM, axis=1)

        nat, p4, p16, kf, vf, zb, ag = _prompt_proj(xp, ng, w_in_l, sg, wsp, bsp, qg, kg, bd)
        a1 = _attn_pass(nat[None], 1)
        a4 = _attn_pass(p4, 4)
        a16 = _attn_pass(p16, 16)
        kp_new.append(kf[tp - keep:].reshape(1, keep, B_HEADS, HEAD_DIM))
        vp_new.append(vf[tp - keep:].reshape(1, keep, B_HEADS, HEAD_DIM))
        xp = _prompt_out(a1, a4, a16, zb, ag, xp, w_out_l)

        w0 = jnp.repeat(w_spatial[l, :, 0, 0], HEAD_DIM)[None, :]
        b0 = jnp.repeat(b_spatial[l, :, 0], HEAD_DIM)[None, :]
        q, k, v, zbs, ags, vn = _sample_proj(xs, ng, w_in_l, sg, w0, b0, qg, kg, bd)
        heads = lambda a: a.reshape(bs, B_HEADS, HEAD_DIM)
        o = _sample_attn(heads(q), heads(k), heads(v), cache_k, cache_v, l)
        ks_new.append(k.reshape(bs, 1, B_HEADS, HEAD_DIM))
        vs_new.append(v.reshape(bs, 1, B_HEADS, HEAD_DIM))
        sgu_new.append(vn.reshape(bs, 1, A_WIDTH))
        xs = _sample_out(o.reshape(bs, B_WIDTH), zbs, ags, xs, w_out_l)

    return (xp.reshape(bp, tp, D_MODEL), xs.reshape(bs, ts, D_MODEL),
            jnp.stack(kp_new, axis=0), jnp.stack(vp_new, axis=0),
            jnp.stack(ks_new, axis=0), jnp.stack(vs_new, axis=0), jnp.stack(sgu_new, axis=0))
```

```python
import functools
import math

import jax
import jax.numpy as jnp
from jax import lax
from jax.experimental import pallas as pl
from jax.experimental.pallas import tpu as pltpu

D_MODEL = 1024
HEAD_DIM = 64
A_WIDTH = 256
A_GROUPS = 4
B_WIDTH = 768
B_HEADS = 12
CHUNK = 128
PATTERNS = ((128, 1), (512, 4), (2048, 16))
WINDOW = 2048
PROJ_WIDTH = 3 * A_WIDTH + 4 * B_WIDTH
EPS = 1e-6
LOG2E = math.log2(math.e)

LANES = 128
BAND = 128
QKV_WIDTH = 3 * B_WIDTH
QKV_SLABS = QKV_WIDTH // LANES
O_SLABS = B_WIDTH // LANES
ACC_SLABS = O_SLABS + 1
NEG_BIG = -1e30
PROJ_TILE = 512
OUT_TILE = 512
ATTN_GROUP = {1: 4, 4: 4, 16: 2}
VMEM_LIMIT = 56 * 1024 * 1024

_UA, _VA, _ZA = 0, A_WIDTH, 2 * A_WIDTH
_Q = 3 * A_WIDTH
_K = _Q + B_WIDTH
_V = _K + B_WIDTH
_ZB = _V + B_WIDTH


def _alibi_slopes():
    return 2.0 ** (-8.0 * jnp.arange(1, B_HEADS + 1, dtype=jnp.float32) / B_HEADS)


def _silu(z):
    return z * (1.0 / (1.0 + jnp.exp(-z)))


def _rms_scale(x):
    return lax.rsqrt(jnp.mean(x * x, axis=-1, keepdims=True) + EPS)


def _head_mean_square(t, bd):
    sq = (t * t).astype(jnp.bfloat16)
    parts = [jnp.dot(sq[:, j * 256:(j + 1) * 256], bd, preferred_element_type=jnp.float32)
             for j in range(B_WIDTH // 256)]
    return jnp.concatenate(parts, axis=1) * (1.0 / HEAD_DIM)


def _project(x, ng, w_ref, sg, qg, kg, bd, q_scale):
    h = (x * _rms_scale(x) * ng).astype(jnp.bfloat16)

    def proj(lo, width):
        return jnp.dot(h, w_ref[:, lo:lo + width], preferred_element_type=jnp.float32)

    ua = proj(_UA, A_WIDTH)
    va = proj(_VA, A_WIDTH)
    za = proj(_ZA, A_WIDTH)
    q = proj(_Q, B_WIDTH)
    k = proj(_K, B_WIDTH)
    v = proj(_V, B_WIDTH)
    zb = proj(_ZB, B_WIDTH)
    vn = va * _rms_scale(va) * sg
    qn = q * lax.rsqrt(_head_mean_square(q, bd) + EPS) * qg * q_scale
    kn = k * lax.rsqrt(_head_mean_square(k, bd) + EPS) * kg
    return ua, vn, za, qn, kn, v, zb


def _prompt_proj_kernel(x_ref, ng_ref, w_ref, sg_ref, wsp_ref, bsp_ref, qg_ref, kg_ref, bd_ref,
                        nat_ref, p4_ref, p16_ref, kf_ref, vf_ref, zb_ref, ag_ref,
                        wtril_scr, nat_scr, c4_scr):
    @pl.when(pl.program_id(0) == 0)
    def _():
        t = lax.broadcasted_iota(jnp.int32, (CHUNK, A_GROUPS * CHUNK), 0)
        s = lax.broadcasted_iota(jnp.int32, (CHUNK, A_GROUPS * CHUNK), 1) % CHUNK
        wtril_scr[...] = jnp.where(s <= t, wsp_ref[...], 0.0).astype(jnp.bfloat16)

    ua, vn, za, qn, kn, v, zb = _project(x_ref[...], ng_ref[...], w_ref, sg_ref[...], qg_ref[...],
                                         kg_ref[...], bd_ref[...], HEAD_DIM ** -0.5 * LOG2E)

    group = lax.broadcasted_iota(jnp.int32, (CHUNK, A_WIDTH), 1) // HEAD_DIM
    vnb = vn.astype(jnp.bfloat16)
    mixed = []
    for c in range(PROJ_TILE // CHUNK):
        vc = vnb[c * CHUNK:(c + 1) * CHUNK]
        rhs = jnp.concatenate([jnp.where(group == g, vc, jnp.zeros_like(vc)) for g in range(A_GROUPS)], axis=0)
        mixed.append(jnp.dot(wtril_scr[...], rhs, preferred_element_type=jnp.float32) + bsp_ref[...])
    mixed = jnp.concatenate(mixed, axis=0)
    ag_ref[...] = (ua * mixed * _silu(za)).astype(ag_ref.dtype)

    kf_ref[...] = kn
    vf_ref[...] = v
    zb_ref[...] = zb
    qkv = jnp.concatenate([qn, kn, v], axis=1)
    nat_ref[...] = qkv.astype(nat_ref.dtype)
    for s in range(QKV_SLABS):
        nat_scr[s] = qkv[:, s * LANES:(s + 1) * LANES]
    rows4 = PROJ_TILE // 4
    rows16 = PROJ_TILE // 16
    for s in range(QKV_SLABS):
        lanes = slice(s * LANES, (s + 1) * LANES)
        for r4 in range(4):
            t4 = nat_scr[s, pl.ds(r4, rows4, stride=4), :]
            p4_ref[r4, :, lanes] = t4.astype(p4_ref.dtype)
            c4_scr[s, r4 * rows4:(r4 + 1) * rows4, :] = t4
        for r4 in range(4):
            for q4 in range(4):
                t16 = c4_scr[s, pl.ds(r4 * rows4 + q4, rows16, stride=4), :]
                p16_ref[r4 + 4 * q4, :, lanes] = t16.astype(p16_ref.dtype)


def _prompt_proj(x, ng, w_in, sg, wsp, bsp, qg, kg, bd, keep):
    t = x.shape[0]
    n = t // PROJ_TILE
    keep_blocks = keep // PROJ_TILE
    const = lambda shape: pl.BlockSpec(shape, lambda i: (0,) * len(shape), pipeline_mode=pl.Buffered(1))
    row = lambda width: pl.BlockSpec((PROJ_TILE, width), lambda i: (i, 0))
    tail = pl.BlockSpec((PROJ_TILE, B_WIDTH), lambda i: (jnp.maximum(i - (n - keep_blocks), 0), 0))
    bf = jnp.bfloat16
    return pl.pallas_call(
        _prompt_proj_kernel,
        grid=(n,),
        in_specs=[row(D_MODEL), const((1, D_MODEL)), const((D_MODEL, PROJ_WIDTH)), const((1, A_WIDTH)),
                  const((CHUNK, A_GROUPS * CHUNK)), const((CHUNK, A_WIDTH)), const((1, B_WIDTH)),
                  const((1, B_WIDTH)), const((256, 256))],
        out_specs=[row(QKV_WIDTH),
                   pl.BlockSpec((4, PROJ_TILE // 4, QKV_WIDTH), lambda i: (0, i, 0)),
                   pl.BlockSpec((16, PROJ_TILE // 16, QKV_WIDTH), lambda i: (0, i, 0)),
                   tail, tail, row(B_WIDTH), row(A_WIDTH)],
        out_shape=[jax.ShapeDtypeStruct((t, QKV_WIDTH), bf),
                   jax.ShapeDtypeStruct((4, t // 4, QKV_WIDTH), bf),
                   jax.ShapeDtypeStruct((16, t // 16, QKV_WIDTH), bf),
                   jax.ShapeDtypeStruct((keep, B_WIDTH), jnp.float32),
                   jax.ShapeDtypeStruct((keep, B_WIDTH), jnp.float32),
                   jax.ShapeDtypeStruct((t, B_WIDTH), jnp.float32),
                   jax.ShapeDtypeStruct((t, A_WIDTH), bf)],
        scratch_shapes=[pltpu.VMEM((CHUNK, A_GROUPS * CHUNK), bf),
                        pltpu.VMEM((QKV_SLABS, PROJ_TILE, LANES), jnp.float32),
                        pltpu.VMEM((QKV_SLABS, PROJ_TILE, LANES), jnp.float32)],
        compiler_params=pltpu.CompilerParams(dimension_semantics=("arbitrary",), vmem_limit_bytes=VMEM_LIMIT),
        name="prompt_proj",
    )(x, ng, w_in, sg, wsp, bsp, qg, kg, bd)


def _band_bias(d):
    a = jnp.arange(BAND)[:, None]
    c = jnp.arange(2 * BAND)[None, :]
    delta = a - c + BAND
    valid = (delta >= 0) & (delta <= BAND)
    dist = (delta * d).astype(jnp.float32)
    pen = -(_alibi_slopes()[:, None, None] * dist[None]) * LOG2E
    general = jnp.where(valid[None], pen, NEG_BIG)
    first = jnp.where((c >= BAND)[None], general, NEG_BIG)
    return jnp.stack([first, general], axis=0)


def _attend_block(q, k, v, bias):
    low = lax.broadcasted_iota(jnp.int32, (1, LANES), 1) < HEAD_DIM
    lane = lax.broadcasted_iota(jnp.int32, (1, LANES), 1)
    lse = jnp.zeros((BAND, LANES), jnp.float32)
    tiles = []
    for hp in range(O_SLABS):
        qp = q[:, hp * LANES:(hp + 1) * LANES]
        kp = k[:, hp * LANES:(hp + 1) * LANES]
        vp = v[:, hp * LANES:(hp + 1) * LANES]
        halves = []
        for half in range(2):
            h = 2 * hp + half
            sel = low if half == 0 else jnp.logical_not(low)
            qh = jnp.where(sel, qp, jnp.zeros_like(qp))
            s = lax.dot_general(qh, kp, (((1,), (1,)), ((), ())), preferred_element_type=jnp.float32)
            s = s + bias(h)
            m = jnp.max(s, axis=-1, keepdims=True)
            p = jnp.exp2(s - m)
            l = jnp.sum(p, axis=-1, keepdims=True)
            o = jnp.dot(p.astype(jnp.bfloat16), vp, preferred_element_type=jnp.float32)
            halves.append(o * (1.0 / l))
            lse = jnp.where(lane == h, m + jnp.log2(l), lse)
        tiles.append(jnp.where(low, halves[0], halves[1]))
    tiles.append(lse)
    return tiles


def _attn_kernel(q_ref, kp_ref, kc_ref, vp_ref, vc_ref, bias_ref, out_ref, *, d, group):
    i = pl.program_id(0)
    r = pl.program_id(1)
    first = jnp.minimum(i, 1)
    for g in range(group):
        q = q_ref[g * BAND:(g + 1) * BAND]
        if g == 0:
            k = jnp.concatenate([kp_ref[...], kc_ref[0:BAND]], axis=0)
            v = jnp.concatenate([vp_ref[...], vc_ref[0:BAND]], axis=0)
            bias = lambda h: bias_ref[first, h]
        else:
            k = kc_ref[(g - 1) * BAND:(g + 1) * BAND]
            v = vc_ref[(g - 1) * BAND:(g + 1) * BAND]
            bias = lambda h: bias_ref[1, h]
        for s, tile in enumerate(_attend_block(q, k, v, bias)):
            if d == 1:
                out_ref[s, g * BAND:(g + 1) * BAND, :] = tile
            else:
                out_ref[s, pl.ds(r + g * BAND * d, BAND, stride=d), :] = tile


def _attn_pass(qkv_d, d):
    group = ATTN_GROUP[d]
    rows = group * BAND
    t = qkv_d.shape[0] * qkv_d.shape[1]
    steps = qkv_d.shape[1] // rows
    cur = lambda col: pl.BlockSpec((None, rows, B_WIDTH), lambda i, r: (r, i, col))
    prev = lambda col: pl.BlockSpec((None, BAND, B_WIDTH), lambda i, r: (r, jnp.maximum(group * i - 1, 0), col))
    return pl.pallas_call(
        functools.partial(_attn_kernel, d=d, group=group),
        grid=(steps, d),
        in_specs=[cur(0), prev(1), cur(1), prev(2), cur(2),
                  pl.BlockSpec((2, B_HEADS, BAND, 2 * BAND), lambda i, r: (0, 0, 0, 0),
                               pipeline_mode=pl.Buffered(1))],
        out_specs=pl.BlockSpec((ACC_SLABS, d * rows, LANES), lambda i, r: (0, i, 0)),
        out_shape=jax.ShapeDtypeStruct((ACC_SLABS, t, LANES), jnp.float32),
        compiler_params=pltpu.CompilerParams(dimension_semantics=("arbitrary", "arbitrary"),
                                             vmem_limit_bytes=VMEM_LIMIT),
        name=f"prompt_attn_d{d}",
    )(qkv_d, qkv_d, qkv_d, qkv_d, qkv_d, _band_bias(d))


def _expand_heads(scale, sel):
    hi = scale.astype(jnp.bfloat16)
    lo = (scale - hi.astype(jnp.float32)).astype(jnp.bfloat16)
    return (jnp.dot(hi, sel, preferred_element_type=jnp.float32)
            + jnp.dot(lo, sel, preferred_element_type=jnp.float32))


def _unpack(ref):
    return jnp.concatenate([ref[s] for s in range(O_SLABS)], axis=1), ref[O_SLABS]


def _prompt_out_kernel(a1_ref, a4_ref, a16_ref, zb_ref, ag_ref, x_ref, w_ref, sel_ref, y_ref):
    lane = lax.broadcasted_iota(jnp.int32, (1, LANES), 1)
    parts = [_unpack(ref) for ref in (a1_ref, a4_ref, a16_ref)]
    m = functools.reduce(jnp.maximum, [lse for _, lse in parts])
    ws = [jnp.exp2(lse - m) for _, lse in parts]
    den = functools.reduce(lambda a, b: a + b, ws)
    o = None
    for w, (tiles, _) in zip(ws, parts):
        scale = jnp.where(lane < B_HEADS, w / den, 0.0)
        term = _expand_heads(scale, sel_ref[...]) * tiles
        o = term if o is None else o + term
    b = (o * _silu(zb_ref[...])).astype(jnp.bfloat16)
    y = x_ref[...] + jnp.dot(ag_ref[...], w_ref[0:A_WIDTH, :], preferred_element_type=jnp.float32)
    y_ref[...] = y + jnp.dot(b, w_ref[A_WIDTH:, :], preferred_element_type=jnp.float32)


def _head_select():
    h = jnp.arange(LANES)[:, None]
    c = jnp.arange(B_WIDTH)[None, :] // HEAD_DIM
    return (h == c).astype(jnp.bfloat16)


def _prompt_out(a1, a4, a16, zb, ag, x, w_out):
    t = x.shape[0]
    acc = pl.BlockSpec((ACC_SLABS, OUT_TILE, LANES), lambda i: (0, i, 0))
    row = lambda width: pl.BlockSpec((OUT_TILE, width), lambda i: (i, 0))
    const = lambda shape: pl.BlockSpec(shape, lambda i: (0, 0))
    return pl.pallas_call(
        _prompt_out_kernel,
        grid=(t // OUT_TILE,),
        in_specs=[acc, acc, acc, row(B_WIDTH), row(A_WIDTH), row(D_MODEL), const((D_MODEL, D_MODEL)),
                  const((LANES, B_WIDTH))],
        out_specs=row(D_MODEL),
        out_shape=jax.ShapeDtypeStruct((t, D_MODEL), jnp.float32),
        compiler_params=pltpu.CompilerParams(dimension_semantics=("arbitrary",), vmem_limit_bytes=VMEM_LIMIT),
        name="prompt_out",
    )(a1, a4, a16, zb, ag, x, w_out, _head_select())


def _sample_proj_kernel(x_ref, ng_ref, w_ref, sg_ref, w0_ref, b0_ref, qg_ref, kg_ref, bd_ref,
                        q_ref, k_ref, v_ref, zb_ref, ag_ref, vn_ref):
    ua, vn, za, qn, kn, v, zb = _project(x_ref[...], ng_ref[...], w_ref, sg_ref[...], qg_ref[...],
                                         kg_ref[...], bd_ref[...], HEAD_DIM ** -0.5)
    mixed = w0_ref[...] * vn + b0_ref[...]
    ag_ref[...] = (ua * mixed * _silu(za)).astype(ag_ref.dtype)
    q_ref[...] = qn
    k_ref[...] = kn
    v_ref[...] = v
    zb_ref[...] = zb
    vn_ref[...] = vn


def _sample_proj(x, ng, w_in, sg, w0, b0, qg, kg, bd):
    n = x.shape[0]
    f32 = jnp.float32
    wide = jax.ShapeDtypeStruct((n, B_WIDTH), f32)
    return pl.pallas_call(
        _sample_proj_kernel,
        out_shape=[wide, wide, wide, wide, jax.ShapeDtypeStruct((n, A_WIDTH), jnp.bfloat16),
                   jax.ShapeDtypeStruct((n, A_WIDTH), f32)],
        compiler_params=pltpu.CompilerParams(vmem_limit_bytes=VMEM_LIMIT),
        name="sample_proj",
    )(x, ng, w_in, sg, w0, b0, qg, kg, bd)


def _window_bias():
    dist = WINDOW - jnp.arange(WINDOW)
    count = sum(((dist % d == 0) & (dist <= win)).astype(jnp.float32) for win, d in PATTERNS)
    pen = -(_alibi_slopes()[:, None] * dist.astype(jnp.float32)[None, :])
    bias = jnp.where((count > 0)[None, :], pen, NEG_BIG)
    return bias[:, None, :], count[None, None, :]


def _sample_attn_kernel(q_ref, kn_ref, vn_ref, kt_ref, vt_ref, bias_ref, count_ref, o_ref):
    q = q_ref[...]
    s_new = jnp.sum(kn_ref[...] * q, axis=1, keepdims=True)
    s = jnp.sum(kt_ref[...] * q, axis=1, keepdims=True) + bias_ref[...]
    m = jnp.maximum(jnp.max(s, axis=-1, keepdims=True), s_new)
    p = count_ref[...] * jnp.exp(s - m)
    p_new = float(len(PATTERNS)) * jnp.exp(s_new - m)
    den = jnp.sum(p, axis=-1, keepdims=True) + p_new
    num = jnp.sum(p * vt_ref[...], axis=-1, keepdims=True) + p_new * vn_ref[...]
    o_ref[...] = num / den


def _sample_attn(q, kn, vn, ck, cv, layer):
    n = q.shape[0]
    tok = pl.BlockSpec((None, B_HEADS, HEAD_DIM, 1), lambda b: (b, 0, 0, 0))
    win = pl.BlockSpec((None, None, B_HEADS, HEAD_DIM, WINDOW), lambda b: (layer, b, 0, 0, 0))
    bias, count = _window_bias()
    return pl.pallas_call(
        _sample_attn_kernel,
        grid=(n,),
        in_specs=[tok, tok, tok, win, win,
                  pl.BlockSpec((B_HEADS, 1, WINDOW), lambda b: (0, 0, 0)),
                  pl.BlockSpec((1, 1, WINDOW), lambda b: (0, 0, 0))],
        out_specs=tok,
        out_shape=jax.ShapeDtypeStruct((n, B_HEADS, HEAD_DIM, 1), jnp.float32),
        compiler_params=pltpu.CompilerParams(dimension_semantics=("arbitrary",), vmem_limit_bytes=VMEM_LIMIT),
        name="sample_attn",
    )(q, kn, vn, ck, cv, bias, count)


def _sample_out_kernel(o_ref, zb_ref, ag_ref, x_ref, w_ref, y_ref):
    b = (o_ref[...] * _silu(zb_ref[...])).astype(jnp.bfloat16)
    y = x_ref[...] + jnp.dot(ag_ref[...], w_ref[0:A_WIDTH, :], preferred_element_type=jnp.float32)
    y_ref[...] = y + jnp.dot(b, w_ref[A_WIDTH:, :], preferred_element_type=jnp.float32)


def _sample_out(o, zb, ag, x, w_out):
    return pl.pallas_call(
        _sample_out_kernel,
        out_shape=jax.ShapeDtypeStruct(x.shape, jnp.float32),
        name="sample_out",
    )(o, zb, ag, x, w_out)


def kernel(x_prompt, x_sample, cache_k, cache_v, norm_g, w_in, sgu_g, w_spatial, b_spatial,
           q_norm_g, k_norm_g, w_out):
    depth = norm_g.shape[0]
    bp, tp, _ = x_prompt.shape
    bs, ts, _ = x_sample.shape
    assert bp == 1 and ts == 1 and cache_k.shape[2] == WINDOW and tp % (16 * BAND * ATTN_GROUP[16]) == 0
    keep = min(WINDOW, tp)
    xp = x_prompt.reshape(tp, D_MODEL)
    xs = x_sample.reshape(bs, D_MODEL)
    bd = (jnp.arange(256)[:, None] // HEAD_DIM == jnp.arange(256)[None, :] // HEAD_DIM).astype(jnp.bfloat16)
    ck = cache_k.transpose(0, 1, 3, 4, 2)
    cv = cache_v.transpose(0, 1, 3, 4, 2)
    kp_new, vp_new, ks_new, vs_new, sgu_new = [], [], [], [], []
    for l in range(depth):
        ng = norm_g[l][None, :]
        sg = sgu_g[l][None, :]
        qg = jnp.tile(q_norm_g[l], B_HEADS)[None, :]
        kg = jnp.tile(k_norm_g[l], B_HEADS)[None, :]
        w_in_l = w_in[l].astype(jnp.bfloat16)
        w_out_l = w_out[l].astype(jnp.bfloat16)
        wsp = w_spatial[l].transpose(1, 0, 2).reshape(CHUNK, A_GROUPS * CHUNK)
        bsp = jnp.repeat(b_spatial[l].T, HEAD_DIM, axis=1)

        nat, p4, p16, kf, vf, zb, ag = _prompt_proj(xp, ng, w_in_l, sg, wsp, bsp, qg, kg, bd, keep)
        a1 = _attn_pass(nat[None], 1)
        a4 = _attn_pass(p4, 4)
        a16 = _attn_pass(p16, 16)
        kp_new.append(kf.reshape(1, keep, B_HEADS, HEAD_DIM))
        vp_new.append(vf.reshape(1, keep, B_HEADS, HEAD_DIM))
        xp = _prompt_out(a1, a4, a16, zb, ag, xp, w_out_l)

        w0 = jnp.repeat(w_spatial[l, :, 0, 0], HEAD_DIM)[None, :]
        b0 = jnp.repeat(b_spatial[l, :, 0], HEAD_DIM)[None, :]
        q, k, v, zbs, ags, vn = _sample_proj(xs, ng, w_in_l, sg, w0, b0, qg, kg, bd)
        column = lambda a: a.reshape(bs, B_HEADS, HEAD_DIM, 1)
        o = _sample_attn(column(q), column(k), column(v), ck, cv, l)
        ks_new.append(k.reshape(bs, 1, B_HEADS, HEAD_DIM))
        vs_new.append(v.reshape(bs, 1, B_HEADS, HEAD_DIM))
        sgu_new.append(vn.reshape(bs, 1, A_WIDTH))
        xs = _sample_out(o.reshape(bs, B_WIDTH), zbs, ags, xs, w_out_l)

    return (xp.reshape(bp, tp, D_MODEL), xs.reshape(bs, ts, D_MODEL),
            jnp.stack(kp_new, axis=0), jnp.stack(vp_new, axis=0),
            jnp.stack(ks_new, axis=0), jnp.stack(vs_new, axis=0), jnp.stack(sgu_new, axis=0))
```

```python
import functools
import math

import jax
import jax.numpy as jnp
from jax import lax
from jax.experimental import pallas as pl
from jax.experimental.pallas import tpu as pltpu

D_MODEL = 1024
HEAD_DIM = 64
A_WIDTH = 256
A_GROUPS = 4
B_WIDTH = 768
B_HEADS = 12
CHUNK = 128
PATTERNS = ((128, 1), (512, 4), (2048, 16))
WINDOW = 2048
PROJ_WIDTH = 3 * A_WIDTH + 4 * B_WIDTH
EPS = 1e-6
LOG2E = math.log2(math.e)

LANES = 128
BAND = 128
QKV_WIDTH = 3 * B_WIDTH
QKV_SLABS = QKV_WIDTH // LANES
O_SLABS = B_WIDTH // LANES
ACC_SLABS = O_SLABS + 1
NEG_BIG = -1e30
PROJ_TILE = 512
OUT_TILE = 512
ATTN_GROUP = {1: 4, 4: 4, 16: 2}
VMEM_LIMIT = 56 * 1024 * 1024

_UA, _VA, _ZA = 0, A_WIDTH, 2 * A_WIDTH
_Q = 3 * A_WIDTH
_K = _Q + B_WIDTH
_V = _K + B_WIDTH
_ZB = _V + B_WIDTH


def _alibi_slopes():
    return 2.0 ** (-8.0 * jnp.arange(1, B_HEADS + 1, dtype=jnp.float32) / B_HEADS)


def _silu(z):
    return z * (1.0 / (1.0 + jnp.exp(-z)))


def _rms_scale(x):
    return lax.rsqrt(jnp.mean(x * x, axis=-1, keepdims=True) + EPS)


def _head_mean_square(t, bd):
    sq = (t * t).astype(jnp.bfloat16)
    parts = [jnp.dot(sq[:, j * 256:(j + 1) * 256], bd, preferred_element_type=jnp.float32)
             for j in range(B_WIDTH // 256)]
    return jnp.concatenate(parts, axis=1) * (1.0 / HEAD_DIM)


def _project(x, ng, w_ref, sg, qg, kg, bd, q_scale):
    h = (x * _rms_scale(x) * ng).astype(jnp.bfloat16)

    def proj(lo, width):
        return jnp.dot(h, w_ref[:, lo:lo + width], preferred_element_type=jnp.float32)

    ua = proj(_UA, A_WIDTH)
    va = proj(_VA, A_WIDTH)
    za = proj(_ZA, A_WIDTH)
    q = proj(_Q, B_WIDTH)
    k = proj(_K, B_WIDTH)
    v = proj(_V, B_WIDTH)
    zb = proj(_ZB, B_WIDTH)
    vn = va * _rms_scale(va) * sg
    qn = q * lax.rsqrt(_head_mean_square(q, bd) + EPS) * qg * q_scale
    kn = k * lax.rsqrt(_head_mean_square(k, bd) + EPS) * kg
    return ua, vn, za, qn, kn, v, zb


def _prompt_proj_kernel(x_ref, ng_ref, w_ref, sg_ref, wsp_ref, bsp_ref, qg_ref, kg_ref, bd_ref,
                        nat_ref, p4_ref, p16_ref, kf_ref, vf_ref, zb_ref, ag_ref,
                        wtril_scr, nat_scr, c4_scr):
    @pl.when(pl.program_id(0) == 0)
    def _():
        t = lax.broadcasted_iota(jnp.int32, (CHUNK, A_GROUPS * CHUNK), 0)
        s = lax.broadcasted_iota(jnp.int32, (CHUNK, A_GROUPS * CHUNK), 1) % CHUNK
        wtril_scr[...] = jnp.where(s <= t, wsp_ref[...], 0.0).astype(jnp.bfloat16)

    ua, vn, za, qn, kn, v, zb = _project(x_ref[...], ng_ref[...], w_ref, sg_ref[...], qg_ref[...],
                                         kg_ref[...], bd_ref[...], HEAD_DIM ** -0.5 * LOG2E)

    group = lax.broadcasted_iota(jnp.int32, (CHUNK, A_WIDTH), 1) // HEAD_DIM
    vnb = vn.astype(jnp.bfloat16)
    mixed = []
    for c in range(PROJ_TILE // CHUNK):
        vc = vnb[c * CHUNK:(c + 1) * CHUNK]
        rhs = jnp.concatenate([jnp.where(group == g, vc, jnp.zeros_like(vc)) for g in range(A_GROUPS)], axis=0)
        mixed.append(jnp.dot(wtril_scr[...], rhs, preferred_element_type=jnp.float32) + bsp_ref[...])
    mixed = jnp.concatenate(mixed, axis=0)
    ag_ref[...] = (ua * mixed * _silu(za)).astype(ag_ref.dtype)

    kf_ref[...] = kn
    vf_ref[...] = v
    zb_ref[...] = _silu(zb).astype(zb_ref.dtype)
    qkv = jnp.concatenate([qn, kn, v], axis=1)
    nat_ref[...] = qkv.astype(nat_ref.dtype)
    for s in range(QKV_SLABS):
        nat_scr[s] = qkv[:, s * LANES:(s + 1) * LANES]
    rows4 = PROJ_TILE // 4
    rows16 = PROJ_TILE // 16
    for s in range(QKV_SLABS):
        lanes = slice(s * LANES, (s + 1) * LANES)
        for r4 in range(4):
            t4 = nat_scr[s, pl.ds(r4, rows4, stride=4), :]
            p4_ref[r4, :, lanes] = t4.astype(p4_ref.dtype)
            c4_scr[s, r4 * rows4:(r4 + 1) * rows4, :] = t4
        for r4 in range(4):
            for q4 in range(4):
                t16 = c4_scr[s, pl.ds(r4 * rows4 + q4, rows16, stride=4), :]
                p16_ref[r4 + 4 * q4, :, lanes] = t16.astype(p16_ref.dtype)


def _prompt_proj(x, ng, w_in, sg, wsp, bsp, qg, kg, bd, keep):
    t = x.shape[0]
    n = t // PROJ_TILE
    keep_blocks = keep // PROJ_TILE
    const = lambda shape: pl.BlockSpec(shape, lambda i: (0,) * len(shape), pipeline_mode=pl.Buffered(1))
    row = lambda width: pl.BlockSpec((PROJ_TILE, width), lambda i: (i, 0))
    tail = pl.BlockSpec((PROJ_TILE, B_WIDTH), lambda i: (jnp.maximum(i - (n - keep_blocks), 0), 0))
    bf = jnp.bfloat16
    return pl.pallas_call(
        _prompt_proj_kernel,
        grid=(n,),
        in_specs=[row(D_MODEL), const((1, D_MODEL)), const((D_MODEL, PROJ_WIDTH)), const((1, A_WIDTH)),
                  const((CHUNK, A_GROUPS * CHUNK)), const((CHUNK, A_WIDTH)), const((1, B_WIDTH)),
                  const((1, B_WIDTH)), const((256, 256))],
        out_specs=[row(QKV_WIDTH),
                   pl.BlockSpec((4, PROJ_TILE // 4, QKV_WIDTH), lambda i: (0, i, 0)),
                   pl.BlockSpec((16, PROJ_TILE // 16, QKV_WIDTH), lambda i: (0, i, 0)),
                   tail, tail, row(B_WIDTH), row(A_WIDTH)],
        out_shape=[jax.ShapeDtypeStruct((t, QKV_WIDTH), bf),
                   jax.ShapeDtypeStruct((4, t // 4, QKV_WIDTH), bf),
                   jax.ShapeDtypeStruct((16, t // 16, QKV_WIDTH), bf),
                   jax.ShapeDtypeStruct((keep, B_WIDTH), jnp.float32),
                   jax.ShapeDtypeStruct((keep, B_WIDTH), jnp.float32),
                   jax.ShapeDtypeStruct((t, B_WIDTH), bf),
                   jax.ShapeDtypeStruct((t, A_WIDTH), bf)],
        scratch_shapes=[pltpu.VMEM((CHUNK, A_GROUPS * CHUNK), bf),
                        pltpu.VMEM((QKV_SLABS, PROJ_TILE, LANES), jnp.float32),
                        pltpu.VMEM((QKV_SLABS, PROJ_TILE, LANES), jnp.float32)],
        compiler_params=pltpu.CompilerParams(dimension_semantics=("arbitrary",), vmem_limit_bytes=VMEM_LIMIT),
        name="prompt_proj",
    )(x, ng, w_in, sg, wsp, bsp, qg, kg, bd)


def _band_bias(d):
    a = jnp.arange(BAND)[:, None]
    c = jnp.arange(2 * BAND)[None, :]
    delta = a - c + BAND
    valid = (delta >= 0) & (delta <= BAND)
    dist = (delta * d).astype(jnp.float32)
    pen = -(_alibi_slopes()[:, None, None] * dist[None]) * LOG2E
    general = jnp.where(valid[None], pen, NEG_BIG)
    first = jnp.where((c >= BAND)[None], general, NEG_BIG)
    return jnp.stack([first, general], axis=0)


def _attend_block(q, k, v, bias):
    low = lax.broadcasted_iota(jnp.int32, (1, LANES), 1) < HEAD_DIM
    lane = lax.broadcasted_iota(jnp.int32, (1, LANES), 1)
    lse = jnp.zeros((BAND, LANES), jnp.float32)
    tiles = []
    for hp in range(O_SLABS):
        qp = q[:, hp * LANES:(hp + 1) * LANES]
        kp = k[:, hp * LANES:(hp + 1) * LANES]
        vp = v[:, hp * LANES:(hp + 1) * LANES]
        halves = []
        for half in range(2):
            h = 2 * hp + half
            sel = low if half == 0 else jnp.logical_not(low)
            qh = jnp.where(sel, qp, jnp.zeros_like(qp))
            s = lax.dot_general(qh, kp, (((1,), (1,)), ((), ())), preferred_element_type=jnp.float32)
            s = s + bias(h)
            m = jnp.max(s, axis=-1, keepdims=True)
            p = jnp.exp2(s - m)
            l = jnp.sum(p, axis=-1, keepdims=True)
            o = jnp.dot(p.astype(jnp.bfloat16), vp, preferred_element_type=jnp.float32)
            halves.append(o * (1.0 / l))
            lse = jnp.where(lane == h, m + jnp.log2(l), lse)
        tiles.append(jnp.where(low, halves[0], halves[1]))
    tiles.append(lse)
    return tiles


def _attn_kernel(q_ref, kp_ref, kc_ref, vp_ref, vc_ref, bias_ref, out_ref, *, d, group):
    i = pl.program_id(0)
    r = pl.program_id(1)
    first = jnp.minimum(i, 1)
    for g in range(group):
        q = q_ref[g * BAND:(g + 1) * BAND]
        if g == 0:
            k = jnp.concatenate([kp_ref[...], kc_ref[0:BAND]], axis=0)
            v = jnp.concatenate([vp_ref[...], vc_ref[0:BAND]], axis=0)
            bias = lambda h: bias_ref[first, h]
        else:
            k = kc_ref[(g - 1) * BAND:(g + 1) * BAND]
            v = vc_ref[(g - 1) * BAND:(g + 1) * BAND]
            bias = lambda h: bias_ref[1, h]
        for s, tile in enumerate(_attend_block(q, k, v, bias)):
            if d == 1:
                out_ref[s, g * BAND:(g + 1) * BAND, :] = tile
            else:
                out_ref[s, pl.ds(r + g * BAND * d, BAND, stride=d), :] = tile


def _attn_pass(qkv_d, d):
    group = ATTN_GROUP[d]
    rows = group * BAND
    t = qkv_d.shape[0] * qkv_d.shape[1]
    steps = qkv_d.shape[1] // rows
    cur = lambda col: pl.BlockSpec((None, rows, B_WIDTH), lambda i, r: (r, i, col))
    prev = lambda col: pl.BlockSpec((None, BAND, B_WIDTH), lambda i, r: (r, jnp.maximum(group * i - 1, 0), col))
    return pl.pallas_call(
        functools.partial(_attn_kernel, d=d, group=group),
        grid=(steps, d),
        in_specs=[cur(0), prev(1), cur(1), prev(2), cur(2),
                  pl.BlockSpec((2, B_HEADS, BAND, 2 * BAND), lambda i, r: (0, 0, 0, 0),
                               pipeline_mode=pl.Buffered(1))],
        out_specs=pl.BlockSpec((ACC_SLABS, d * rows, LANES), lambda i, r: (0, i, 0)),
        out_shape=jax.ShapeDtypeStruct((ACC_SLABS, t, LANES), jnp.float32),
        compiler_params=pltpu.CompilerParams(dimension_semantics=("arbitrary", "arbitrary"),
                                             vmem_limit_bytes=VMEM_LIMIT),
        name=f"prompt_attn_d{d}",
    )(qkv_d, qkv_d, qkv_d, qkv_d, qkv_d, _band_bias(d))


def _expand_heads(scale, sel):
    hi = scale.astype(jnp.bfloat16)
    lo = (scale - hi.astype(jnp.float32)).astype(jnp.bfloat16)
    return (jnp.dot(hi, sel, preferred_element_type=jnp.float32)
            + jnp.dot(lo, sel, preferred_element_type=jnp.float32))


def _unpack(ref):
    return jnp.concatenate([ref[s] for s in range(O_SLABS)], axis=1), ref[O_SLABS]


def _prompt_out_kernel(a1_ref, a4_ref, a16_ref, gate_ref, ag_ref, x_ref, w_ref, sel_ref, y_ref):
    lane = lax.broadcasted_iota(jnp.int32, (1, LANES), 1)
    parts = [_unpack(ref) for ref in (a1_ref, a4_ref, a16_ref)]
    m = functools.reduce(jnp.maximum, [lse for _, lse in parts])
    ws = [jnp.exp2(lse - m) for _, lse in parts]
    den = functools.reduce(lambda a, b: a + b, ws)
    o = None
    for w, (tiles, _) in zip(ws, parts):
        scale = jnp.where(lane < B_HEADS, w / den, 0.0)
        term = _expand_heads(scale, sel_ref[...]) * tiles
        o = term if o is None else o + term
    b = (o * gate_ref[...].astype(jnp.float32)).astype(jnp.bfloat16)
    y = x_ref[...] + jnp.dot(ag_ref[...], w_ref[0:A_WIDTH, :], preferred_element_type=jnp.float32)
    y_ref[...] = y + jnp.dot(b, w_ref[A_WIDTH:, :], preferred_element_type=jnp.float32)


def _head_select():
    h = jnp.arange(LANES)[:, None]
    c = jnp.arange(B_WIDTH)[None, :] // HEAD_DIM
    return (h == c).astype(jnp.bfloat16)


def _prompt_out(a1, a4, a16, zb, ag, x, w_out):
    t = x.shape[0]
    acc = pl.BlockSpec((ACC_SLABS, OUT_TILE, LANES), lambda i: (0, i, 0))
    row = lambda width: pl.BlockSpec((OUT_TILE, width), lambda i: (i, 0))
    const = lambda shape: pl.BlockSpec(shape, lambda i: (0, 0))
    return pl.pallas_call(
        _prompt_out_kernel,
        grid=(t // OUT_TILE,),
        in_specs=[acc, acc, acc, row(B_WIDTH), row(A_WIDTH), row(D_MODEL), const((D_MODEL, D_MODEL)),
                  const((LANES, B_WIDTH))],
        out_specs=row(D_MODEL),
        out_shape=jax.ShapeDtypeStruct((t, D_MODEL), jnp.float32),
        compiler_params=pltpu.CompilerParams(dimension_semantics=("arbitrary",), vmem_limit_bytes=VMEM_LIMIT),
        name="prompt_out",
    )(a1, a4, a16, zb, ag, x, w_out, _head_select())


def _transpose_by_identity(eye, t):
    out = None
    rem = t
    for _ in range(3):
        piece = rem.astype(jnp.bfloat16)
        rem = rem - piece.astype(jnp.float32)
        term = lax.dot_general(eye, piece, (((1,), (1,)), ((), ())), preferred_element_type=jnp.float32)
        out = term if out is None else out + term
    return out


def _sample_proj_kernel(x_ref, ng_ref, w_ref, sg_ref, w0_ref, b0_ref, qg_ref, kg_ref, bd_ref, eye_ref,
                        qt_ref, kt_ref, vt_ref, k_ref, v_ref, zb_ref, ag_ref, vn_ref):
    ua, vn, za, qn, kn, v, zb = _project(x_ref[...], ng_ref[...], w_ref, sg_ref[...], qg_ref[...],
                                         kg_ref[...], bd_ref[...], HEAD_DIM ** -0.5)
    mixed = w0_ref[...] * vn + b0_ref[...]
    ag_ref[...] = (ua * mixed * _silu(za)).astype(ag_ref.dtype)
    qt_ref[...] = _transpose_by_identity(eye_ref[...], qn)
    kt_ref[...] = _transpose_by_identity(eye_ref[...], kn)
    vt_ref[...] = _transpose_by_identity(eye_ref[...], v)
    k_ref[...] = kn
    v_ref[...] = v
    zb_ref[...] = zb
    vn_ref[...] = vn


def _sample_proj(x, ng, w_in, sg, w0, b0, qg, kg, bd):
    n = x.shape[0]
    f32 = jnp.float32
    wide = jax.ShapeDtypeStruct((n, B_WIDTH), f32)
    tall = jax.ShapeDtypeStruct((B_WIDTH, n), f32)
    return pl.pallas_call(
        _sample_proj_kernel,
        out_shape=[tall, tall, tall, wide, wide, wide, jax.ShapeDtypeStruct((n, A_WIDTH), jnp.bfloat16),
                   jax.ShapeDtypeStruct((n, A_WIDTH), f32)],
        compiler_params=pltpu.CompilerParams(vmem_limit_bytes=VMEM_LIMIT),
        name="sample_proj",
    )(x, ng, w_in, sg, w0, b0, qg, kg, bd, jnp.eye(B_WIDTH, dtype=jnp.bfloat16))


def _window_bias():
    dist = WINDOW - jnp.arange(WINDOW)
    count = sum(((dist % d == 0) & (dist <= win)).astype(jnp.float32) for win, d in PATTERNS)
    pen = -(_alibi_slopes()[:, None] * dist.astype(jnp.float32)[None, :])
    bias = jnp.where((count > 0)[None, :], pen, NEG_BIG)
    return bias[:, None, :], count[None, None, :]


def _sample_attn_kernel(q_ref, kn_ref, vn_ref, kt_ref, vt_ref, bias_ref, count_ref, o_ref):
    b = pl.program_id(0)
    mine = lax.broadcasted_iota(jnp.int32, (1, 1, q_ref.shape[-1]), 2) == b
    pick = lambda ref: jnp.sum(jnp.where(mine, ref[...], 0.0), axis=-1, keepdims=True)
    q = pick(q_ref)
    k_new = pick(kn_ref)
    v_new = pick(vn_ref)
    s_new = jnp.sum(k_new * q, axis=1, keepdims=True)
    s = jnp.sum(kt_ref[...] * q, axis=1, keepdims=True) + bias_ref[...]
    m = jnp.maximum(jnp.max(s, axis=-1, keepdims=True), s_new)
    p = count_ref[...] * jnp.exp(s - m)
    p_new = float(len(PATTERNS)) * jnp.exp(s_new - m)
    den = jnp.sum(p, axis=-1, keepdims=True) + p_new
    num = jnp.sum(p * vt_ref[...], axis=-1, keepdims=True) + p_new * v_new

    @pl.when(b == 0)
    def _():
        o_ref[...] = jnp.zeros_like(o_ref)

    o_ref[...] = jnp.where(mine, num / den, o_ref[...])


def _sample_attn(q, kn, vn, ck, cv, layer):
    n = q.shape[-1]
    tok = pl.BlockSpec((B_HEADS, HEAD_DIM, n), lambda b: (0, 0, 0))
    win = pl.BlockSpec((None, None, B_HEADS, HEAD_DIM, WINDOW), lambda b: (layer, b, 0, 0, 0))
    bias, count = _window_bias()
    return pl.pallas_call(
        _sample_attn_kernel,
        grid=(n,),
        in_specs=[tok, tok, tok, win, win,
                  pl.BlockSpec((B_HEADS, 1, WINDOW), lambda b: (0, 0, 0)),
                  pl.BlockSpec((1, 1, WINDOW), lambda b: (0, 0, 0))],
        out_specs=tok,
        out_shape=jax.ShapeDtypeStruct((B_HEADS, HEAD_DIM, n), jnp.float32),
        compiler_params=pltpu.CompilerParams(dimension_semantics=("arbitrary",), vmem_limit_bytes=VMEM_LIMIT),
        name="sample_attn",
    )(q, kn, vn, ck, cv, bias, count)


def _sample_out_kernel(ot_ref, zb_ref, ag_ref, x_ref, w_ref, eye_ref, y_ref):
    o = _transpose_by_identity(eye_ref[...], ot_ref[...])
    b = (o * _silu(zb_ref[...])).astype(jnp.bfloat16)
    y = x_ref[...] + jnp.dot(ag_ref[...], w_ref[0:A_WIDTH, :], preferred_element_type=jnp.float32)
    y_ref[...] = y + jnp.dot(b, w_ref[A_WIDTH:, :], preferred_element_type=jnp.float32)


def _sample_out(ot, zb, ag, x, w_out):
    return pl.pallas_call(
        _sample_out_kernel,
        out_shape=jax.ShapeDtypeStruct(x.shape, jnp.float32),
        name="sample_out",
    )(ot, zb, ag, x, w_out, jnp.eye(x.shape[0], dtype=jnp.bfloat16))


def kernel(x_prompt, x_sample, cache_k, cache_v, norm_g, w_in, sgu_g, w_spatial, b_spatial,
           q_norm_g, k_norm_g, w_out):
    depth = norm_g.shape[0]
    bp, tp, _ = x_prompt.shape
    bs, ts, _ = x_sample.shape
    assert bp == 1 and ts == 1 and cache_k.shape[2] == WINDOW and tp % (16 * BAND * ATTN_GROUP[16]) == 0
    keep = min(WINDOW, tp)
    xp = x_prompt.reshape(tp, D_MODEL)
    xs = x_sample.reshape(bs, D_MODEL)
    bd = (jnp.arange(256)[:, None] // HEAD_DIM == jnp.arange(256)[None, :] // HEAD_DIM).astype(jnp.bfloat16)
    ck = cache_k.transpose(0, 1, 3, 4, 2)
    cv = cache_v.transpose(0, 1, 3, 4, 2)
    kp_new, vp_new, ks_new, vs_new, sgu_new = [], [], [], [], []
    for l in range(depth):
        ng = norm_g[l][None, :]
        sg = sgu_g[l][None, :]
        qg = jnp.tile(q_norm_g[l], B_HEADS)[None, :]
        kg = jnp.tile(k_norm_g[l], B_HEADS)[None, :]
        w_in_l = w_in[l].astype(jnp.bfloat16)
        w_out_l = w_out[l].astype(jnp.bfloat16)
        wsp = w_spatial[l].transpose(1, 0, 2).reshape(CHUNK, A_GROUPS * CHUNK)
        bsp = jnp.repeat(b_spatial[l].T, HEAD_DIM, axis=1)

        nat, p4, p16, kf, vf, zb, ag = _prompt_proj(xp, ng, w_in_l, sg, wsp, bsp, qg, kg, bd, keep)
        a1 = _attn_pass(nat[None], 1)
        a4 = _attn_pass(p4, 4)
        a16 = _attn_pass(p16, 16)
        kp_new.append(kf.reshape(1, keep, B_HEADS, HEAD_DIM))
        vp_new.append(vf.reshape(1, keep, B_HEADS, HEAD_DIM))
        xp = _prompt_out(a1, a4, a16, zb, ag, xp, w_out_l)

        w0 = jnp.repeat(w_spatial[l, :, 0, 0], HEAD_DIM)[None, :]
        b0 = jnp.repeat(b_spatial[l, :, 0], HEAD_DIM)[None, :]
        qt, kt, vt, k, v, zbs, ags, vn = _sample_proj(xs, ng, w_in_l, sg, w0, b0, qg, kg, bd)
        heads = lambda a: a.reshape(B_HEADS, HEAD_DIM, bs)
        ot = _sample_attn(heads(qt), heads(kt), heads(vt), ck, cv, l)
        ks_new.append(k.reshape(bs, 1, B_HEADS, HEAD_DIM))
        vs_new.append(v.reshape(bs, 1, B_HEADS, HEAD_DIM))
        sgu_new.append(vn.reshape(bs, 1, A_WIDTH))
        xs = _sample_out(ot.reshape(B_WIDTH, bs), zbs, ags, xs, w_out_l)

    return (xp.reshape(bp, tp, D_MODEL), xs.reshape(bs, ts, D_MODEL),
            jnp.stack(kp_new, axis=0), jnp.stack(vp_new, axis=0),
            jnp.stack(ks_new, axis=0), jnp.stack(vs_new, axis=0), jnp.stack(sgu_new, axis=0))
```

```python
import functools
import math

import jax
import jax.numpy as jnp
from jax import lax
from jax.experimental import pallas as pl
from jax.experimental.pallas import tpu as pltpu

D_MODEL = 1024
HEAD_DIM = 64
A_WIDTH = 256
A_GROUPS = 4
B_WIDTH = 768
B_HEADS = 12
CHUNK = 128
PATTERNS = ((128, 1), (512, 4), (2048, 16))
WINDOW = 2048
PROJ_WIDTH = 3 * A_WIDTH + 4 * B_WIDTH
EPS = 1e-6
LOG2E = math.log2(math.e)

LANES = 128
BAND = 128
QKV_WIDTH = 3 * B_WIDTH
QKV_SLABS = QKV_WIDTH // LANES
O_SLABS = B_WIDTH // LANES
ACC_SLABS = O_SLABS + 1
NEG_BIG = -1e30
PROJ_TILE = 512
OUT_TILE = 1024
ATTN_GROUP = {1: 8, 4: 8, 16: 2}
VMEM_LIMIT = 56 * 1024 * 1024

_UA, _VA, _ZA = 0, A_WIDTH, 2 * A_WIDTH
_Q = 3 * A_WIDTH
_K = _Q + B_WIDTH
_V = _K + B_WIDTH
_ZB = _V + B_WIDTH


def _alibi_slopes():
    return 2.0 ** (-8.0 * jnp.arange(1, B_HEADS + 1, dtype=jnp.float32) / B_HEADS)


def _silu(z):
    return z * (1.0 / (1.0 + jnp.exp(-z)))


def _rms_scale(x):
    return lax.rsqrt(jnp.mean(x * x, axis=-1, keepdims=True) + EPS)


def _head_mean_square(t, bd):
    sq = (t * t).astype(jnp.bfloat16)
    parts = [jnp.dot(sq[:, j * 256:(j + 1) * 256], bd, preferred_element_type=jnp.float32)
             for j in range(B_WIDTH // 256)]
    return jnp.concatenate(parts, axis=1) * (1.0 / HEAD_DIM)


def _project(x, ng, w_ref, sg, qg, kg, bd, q_scale):
    h = (x * _rms_scale(x) * ng).astype(jnp.bfloat16)

    def proj(lo, width):
        return jnp.dot(h, w_ref[:, lo:lo + width], preferred_element_type=jnp.float32)

    ua = proj(_UA, A_WIDTH)
    va = proj(_VA, A_WIDTH)
    za = proj(_ZA, A_WIDTH)
    q = proj(_Q, B_WIDTH)
    k = proj(_K, B_WIDTH)
    v = proj(_V, B_WIDTH)
    zb = proj(_ZB, B_WIDTH)
    vn = va * _rms_scale(va) * sg
    qn = q * lax.rsqrt(_head_mean_square(q, bd) + EPS) * qg * q_scale
    kn = k * lax.rsqrt(_head_mean_square(k, bd) + EPS) * kg
    return ua, vn, za, qn, kn, v, zb


def _prompt_proj_kernel(x_ref, ng_ref, w_ref, sg_ref, wsp_ref, bsp_ref, qg_ref, kg_ref, bd_ref,
                        nat_ref, p4_ref, p16_ref, kf_ref, vf_ref, zb_ref, ag_ref,
                        wtril_scr, nat_scr, c4_scr):
    @pl.when(pl.program_id(0) == 0)
    def _():
        t = lax.broadcasted_iota(jnp.int32, (CHUNK, A_GROUPS * CHUNK), 0)
        s = lax.broadcasted_iota(jnp.int32, (CHUNK, A_GROUPS * CHUNK), 1) % CHUNK
        wtril_scr[...] = jnp.where(s <= t, wsp_ref[...], 0.0).astype(jnp.bfloat16)

    ua, vn, za, qn, kn, v, zb = _project(x_ref[...], ng_ref[...], w_ref, sg_ref[...], qg_ref[...],
                                         kg_ref[...], bd_ref[...], HEAD_DIM ** -0.5 * LOG2E)

    group = lax.broadcasted_iota(jnp.int32, (CHUNK, A_WIDTH), 1) // HEAD_DIM
    vnb = vn.astype(jnp.bfloat16)
    mixed = []
    for c in range(PROJ_TILE // CHUNK):
        vc = vnb[c * CHUNK:(c + 1) * CHUNK]
        rhs = jnp.concatenate([jnp.where(group == g, vc, jnp.zeros_like(vc)) for g in range(A_GROUPS)], axis=0)
        mixed.append(jnp.dot(wtril_scr[...], rhs, preferred_element_type=jnp.float32) + bsp_ref[...])
    mixed = jnp.concatenate(mixed, axis=0)
    ag_ref[...] = (ua * mixed * _silu(za)).astype(ag_ref.dtype)

    kf_ref[...] = kn
    vf_ref[...] = v
    zb_ref[...] = _silu(zb).astype(zb_ref.dtype)
    qkv = jnp.concatenate([qn, kn, v], axis=1)
    nat_ref[...] = qkv.astype(nat_ref.dtype)
    for s in range(QKV_SLABS):
        nat_scr[s] = qkv[:, s * LANES:(s + 1) * LANES]
    rows4 = PROJ_TILE // 4
    rows16 = PROJ_TILE // 16
    for s in range(QKV_SLABS):
        lanes = slice(s * LANES, (s + 1) * LANES)
        for r4 in range(4):
            t4 = nat_scr[s, pl.ds(r4, rows4, stride=4), :]
            p4_ref[r4, :, lanes] = t4.astype(p4_ref.dtype)
            c4_scr[s, r4 * rows4:(r4 + 1) * rows4, :] = t4
        for r4 in range(4):
            for q4 in range(4):
                t16 = c4_scr[s, pl.ds(r4 * rows4 + q4, rows16, stride=4), :]
                p16_ref[r4 + 4 * q4, :, lanes] = t16.astype(p16_ref.dtype)


def _prompt_proj(x, ng, w_in, sg, wsp, bsp, qg, kg, bd, keep):
    t = x.shape[0]
    n = t // PROJ_TILE
    keep_blocks = keep // PROJ_TILE
    const = lambda shape: pl.BlockSpec(shape, lambda i: (0,) * len(shape), pipeline_mode=pl.Buffered(1))
    row = lambda width: pl.BlockSpec((PROJ_TILE, width), lambda i: (i, 0))
    tail = pl.BlockSpec((PROJ_TILE, B_WIDTH), lambda i: (jnp.maximum(i - (n - keep_blocks), 0), 0))
    bf = jnp.bfloat16
    return pl.pallas_call(
        _prompt_proj_kernel,
        grid=(n,),
        in_specs=[row(D_MODEL), const((1, D_MODEL)), const((D_MODEL, PROJ_WIDTH)), const((1, A_WIDTH)),
                  const((CHUNK, A_GROUPS * CHUNK)), const((CHUNK, A_WIDTH)), const((1, B_WIDTH)),
                  const((1, B_WIDTH)), const((256, 256))],
        out_specs=[row(QKV_WIDTH),
                   pl.BlockSpec((4, PROJ_TILE // 4, QKV_WIDTH), lambda i: (0, i, 0)),
                   pl.BlockSpec((16, PROJ_TILE // 16, QKV_WIDTH), lambda i: (0, i, 0)),
                   tail, tail, row(B_WIDTH), row(A_WIDTH)],
        out_shape=[jax.ShapeDtypeStruct((t, QKV_WIDTH), bf),
                   jax.ShapeDtypeStruct((4, t // 4, QKV_WIDTH), bf),
                   jax.ShapeDtypeStruct((16, t // 16, QKV_WIDTH), bf),
                   jax.ShapeDtypeStruct((keep, B_WIDTH), jnp.float32),
                   jax.ShapeDtypeStruct((keep, B_WIDTH), jnp.float32),
                   jax.ShapeDtypeStruct((t, B_WIDTH), bf),
                   jax.ShapeDtypeStruct((t, A_WIDTH), bf)],
        scratch_shapes=[pltpu.VMEM((CHUNK, A_GROUPS * CHUNK), bf),
                        pltpu.VMEM((QKV_SLABS, PROJ_TILE, LANES), jnp.float32),
                        pltpu.VMEM((QKV_SLABS, PROJ_TILE, LANES), jnp.float32)],
        compiler_params=pltpu.CompilerParams(dimension_semantics=("arbitrary",), vmem_limit_bytes=VMEM_LIMIT),
        name="prompt_proj",
    )(x, ng, w_in, sg, wsp, bsp, qg, kg, bd)


def _band_bias(d):
    a = jnp.arange(BAND)[:, None]
    c = jnp.arange(2 * BAND)[None, :]
    delta = a - c + BAND
    valid = (delta >= 0) & (delta <= BAND)
    dist = (delta * d).astype(jnp.float32)
    pen = -(_alibi_slopes()[:, None, None] * dist[None]) * LOG2E
    general = jnp.where(valid[None], pen, NEG_BIG)
    first = jnp.where((c >= BAND)[None], general, NEG_BIG)
    return jnp.stack([first, general], axis=0)


def _attend_block(q, k, v, bias):
    low = lax.broadcasted_iota(jnp.int32, (1, LANES), 1) < HEAD_DIM
    lane = lax.broadcasted_iota(jnp.int32, (1, LANES), 1)
    lse = jnp.zeros((BAND, LANES), jnp.float32)
    tiles = []
    for hp in range(O_SLABS):
        qp = q[:, hp * LANES:(hp + 1) * LANES]
        kp = k[:, hp * LANES:(hp + 1) * LANES]
        vp = v[:, hp * LANES:(hp + 1) * LANES]
        halves = []
        for half in range(2):
            h = 2 * hp + half
            sel = low if half == 0 else jnp.logical_not(low)
            qh = jnp.where(sel, qp, jnp.zeros_like(qp))
            s = lax.dot_general(qh, kp, (((1,), (1,)), ((), ())), preferred_element_type=jnp.float32)
            s = s + bias(h)
            m = jnp.max(s, axis=-1, keepdims=True)
            p = jnp.exp2(s - m)
            l = jnp.sum(p, axis=-1, keepdims=True)
            o = jnp.dot(p.astype(jnp.bfloat16), vp, preferred_element_type=jnp.float32)
            halves.append(o * (1.0 / l))
            lse = jnp.where(lane == h, m + jnp.log2(l), lse)
        tiles.append(jnp.where(low, halves[0], halves[1]))
    tiles.append(lse)
    return tiles


def _attn_kernel(q_ref, kp_ref, kc_ref, vp_ref, vc_ref, bias_ref, out_ref, *, d, group):
    i = pl.program_id(0)
    r = pl.program_id(1)
    first = jnp.minimum(i, 1)
    for g in range(group):
        q = q_ref[g * BAND:(g + 1) * BAND]
        if g == 0:
            k = jnp.concatenate([kp_ref[...], kc_ref[0:BAND]], axis=0)
            v = jnp.concatenate([vp_ref[...], vc_ref[0:BAND]], axis=0)
            bias = lambda h: bias_ref[first, h]
        else:
            k = kc_ref[(g - 1) * BAND:(g + 1) * BAND]
            v = vc_ref[(g - 1) * BAND:(g + 1) * BAND]
            bias = lambda h: bias_ref[1, h]
        for s, tile in enumerate(_attend_block(q, k, v, bias)):
            if d == 1:
                out_ref[s, g * BAND:(g + 1) * BAND, :] = tile
            else:
                out_ref[s, pl.ds(r + g * BAND * d, BAND, stride=d), :] = tile


def _attn_pass(qkv_d, d):
    group = ATTN_GROUP[d]
    rows = group * BAND
    t = qkv_d.shape[0] * qkv_d.shape[1]
    steps = qkv_d.shape[1] // rows
    cur = lambda col: pl.BlockSpec((None, rows, B_WIDTH), lambda i, r: (r, i, col))
    prev = lambda col: pl.BlockSpec((None, BAND, B_WIDTH), lambda i, r: (r, jnp.maximum(group * i - 1, 0), col))
    return pl.pallas_call(
        functools.partial(_attn_kernel, d=d, group=group),
        grid=(steps, d),
        in_specs=[cur(0), prev(1), cur(1), prev(2), cur(2),
                  pl.BlockSpec((2, B_HEADS, BAND, 2 * BAND), lambda i, r: (0, 0, 0, 0),
                               pipeline_mode=pl.Buffered(1))],
        out_specs=pl.BlockSpec((ACC_SLABS, d * rows, LANES), lambda i, r: (0, i, 0)),
        out_shape=jax.ShapeDtypeStruct((ACC_SLABS, t, LANES), jnp.float32),
        compiler_params=pltpu.CompilerParams(dimension_semantics=("arbitrary", "arbitrary"),
                                             vmem_limit_bytes=VMEM_LIMIT),
        name=f"prompt_attn_d{d}",
    )(qkv_d, qkv_d, qkv_d, qkv_d, qkv_d, _band_bias(d))


def _expand_heads(scale, sel):
    hi = scale.astype(jnp.bfloat16)
    lo = (scale - hi.astype(jnp.float32)).astype(jnp.bfloat16)
    return (jnp.dot(hi, sel, preferred_element_type=jnp.float32)
            + jnp.dot(lo, sel, preferred_element_type=jnp.float32))


def _unpack(ref):
    return jnp.concatenate([ref[s] for s in range(O_SLABS)], axis=1), ref[O_SLABS]


def _prompt_out_kernel(a1_ref, a4_ref, a16_ref, gate_ref, ag_ref, x_ref, w_ref, sel_ref, y_ref):
    lane = lax.broadcasted_iota(jnp.int32, (1, LANES), 1)
    parts = [_unpack(ref) for ref in (a1_ref, a4_ref, a16_ref)]
    m = functools.reduce(jnp.maximum, [lse for _, lse in parts])
    ws = [jnp.exp2(lse - m) for _, lse in parts]
    den = functools.reduce(lambda a, b: a + b, ws)
    o = None
    for w, (tiles, _) in zip(ws, parts):
        scale = jnp.where(lane < B_HEADS, w / den, 0.0)
        term = _expand_heads(scale, sel_ref[...]) * tiles
        o = term if o is None else o + term
    b = (o * gate_ref[...].astype(jnp.float32)).astype(jnp.bfloat16)
    y = x_ref[...] + jnp.dot(ag_ref[...], w_ref[0:A_WIDTH, :], preferred_element_type=jnp.float32)
    y_ref[...] = y + jnp.dot(b, w_ref[A_WIDTH:, :], preferred_element_type=jnp.float32)


def _head_select():
    h = jnp.arange(LANES)[:, None]
    c = jnp.arange(B_WIDTH)[None, :] // HEAD_DIM
    return (h == c).astype(jnp.bfloat16)


def _prompt_out(a1, a4, a16, zb, ag, x, w_out):
    t = x.shape[0]
    acc = pl.BlockSpec((ACC_SLABS, OUT_TILE, LANES), lambda i: (0, i, 0))
    row = lambda width: pl.BlockSpec((OUT_TILE, width), lambda i: (i, 0))
    const = lambda shape: pl.BlockSpec(shape, lambda i: (0, 0))
    return pl.pallas_call(
        _prompt_out_kernel,
        grid=(t // OUT_TILE,),
        in_specs=[acc, acc, acc, row(B_WIDTH), row(A_WIDTH), row(D_MODEL), const((D_MODEL, D_MODEL)),
                  const((LANES, B_WIDTH))],
        out_specs=row(D_MODEL),
        out_shape=jax.ShapeDtypeStruct((t, D_MODEL), jnp.float32),
        compiler_params=pltpu.CompilerParams(dimension_semantics=("arbitrary",), vmem_limit_bytes=VMEM_LIMIT),
        name="prompt_out",
    )(a1, a4, a16, zb, ag, x, w_out, _head_select())


def _transpose_by_identity(eye, t):
    out = None
    rem = t
    for _ in range(3):
        piece = rem.astype(jnp.bfloat16)
        rem = rem - piece.astype(jnp.float32)
        term = lax.dot_general(eye, piece, (((1,), (1,)), ((), ())), preferred_element_type=jnp.float32)
        out = term if out is None else out + term
    return out


def _sample_proj_kernel(x_ref, ng_ref, w_ref, sg_ref, w0_ref, b0_ref, qg_ref, kg_ref, bd_ref, eye_ref,
                        qt_ref, kt_ref, vt_ref, k_ref, v_ref, zb_ref, ag_ref, vn_ref):
    ua, vn, za, qn, kn, v, zb = _project(x_ref[...], ng_ref[...], w_ref, sg_ref[...], qg_ref[...],
                                         kg_ref[...], bd_ref[...], HEAD_DIM ** -0.5)
    mixed = w0_ref[...] * vn + b0_ref[...]
    ag_ref[...] = (ua * mixed * _silu(za)).astype(ag_ref.dtype)
    qt_ref[...] = _transpose_by_identity(eye_ref[...], qn)
    kt_ref[...] = _transpose_by_identity(eye_ref[...], kn)
    vt_ref[...] = _transpose_by_identity(eye_ref[...], v)
    k_ref[...] = kn
    v_ref[...] = v
    zb_ref[...] = zb
    vn_ref[...] = vn


def _sample_proj(x, ng, w_in, sg, w0, b0, qg, kg, bd):
    n = x.shape[0]
    f32 = jnp.float32
    wide = jax.ShapeDtypeStruct((n, B_WIDTH), f32)
    tall = jax.ShapeDtypeStruct((B_WIDTH, n), f32)
    return pl.pallas_call(
        _sample_proj_kernel,
        out_shape=[tall, tall, tall, wide, wide, wide, jax.ShapeDtypeStruct((n, A_WIDTH), jnp.bfloat16),
                   jax.ShapeDtypeStruct((n, A_WIDTH), f32)],
        compiler_params=pltpu.CompilerParams(vmem_limit_bytes=VMEM_LIMIT),
        name="sample_proj",
    )(x, ng, w_in, sg, w0, b0, qg, kg, bd, jnp.eye(B_WIDTH, dtype=jnp.bfloat16))


def _window_bias():
    dist = WINDOW - jnp.arange(WINDOW)
    count = sum(((dist % d == 0) & (dist <= win)).astype(jnp.float32) for win, d in PATTERNS)
    pen = -(_alibi_slopes()[:, None] * dist.astype(jnp.float32)[None, :])
    bias = jnp.where((count > 0)[None, :], pen, NEG_BIG)
    return bias[:, None, :], count[None, None, :]


def _sample_attn_kernel(q_ref, kn_ref, vn_ref, kt_ref, vt_ref, bias_ref, count_ref, o_ref):
    b = pl.program_id(0)
    mine = lax.broadcasted_iota(jnp.int32, (1, 1, q_ref.shape[-1]), 2) == b
    pick = lambda ref: jnp.sum(jnp.where(mine, ref[...], 0.0), axis=-1, keepdims=True)
    q = pick(q_ref)
    k_new = pick(kn_ref)
    v_new = pick(vn_ref)
    s_new = jnp.sum(k_new * q, axis=1, keepdims=True)
    s = jnp.sum(kt_ref[...] * q, axis=1, keepdims=True) + bias_ref[...]
    m = jnp.maximum(jnp.max(s, axis=-1, keepdims=True), s_new)
    p = count_ref[...] * jnp.exp(s - m)
    p_new = float(len(PATTERNS)) * jnp.exp(s_new - m)
    den = jnp.sum(p, axis=-1, keepdims=True) + p_new
    num = jnp.sum(p * vt_ref[...], axis=-1, keepdims=True) + p_new * v_new

    @pl.when(b == 0)
    def _():
        o_ref[...] = jnp.zeros_like(o_ref)

    o_ref[...] = jnp.where(mine, num / den, o_ref[...])


def _sample_attn(q, kn, vn, ck, cv, layer):
    n = q.shape[-1]
    tok = pl.BlockSpec((B_HEADS, HEAD_DIM, n), lambda b: (0, 0, 0))
    win = pl.BlockSpec((None, None, B_HEADS, HEAD_DIM, WINDOW), lambda b: (layer, b, 0, 0, 0))
    bias, count = _window_bias()
    return pl.pallas_call(
        _sample_attn_kernel,
        grid=(n,),
        in_specs=[tok, tok, tok, win, win,
                  pl.BlockSpec((B_HEADS, 1, WINDOW), lambda b: (0, 0, 0)),
                  pl.BlockSpec((1, 1, WINDOW), lambda b: (0, 0, 0))],
        out_specs=tok,
        out_shape=jax.ShapeDtypeStruct((B_HEADS, HEAD_DIM, n), jnp.float32),
        compiler_params=pltpu.CompilerParams(dimension_semantics=("arbitrary",), vmem_limit_bytes=VMEM_LIMIT),
        name="sample_attn",
    )(q, kn, vn, ck, cv, bias, count)


def _sample_out_kernel(ot_ref, zb_ref, ag_ref, x_ref, w_ref, eye_ref, y_ref):
    o = _transpose_by_identity(eye_ref[...], ot_ref[...])
    b = (o * _silu(zb_ref[...])).astype(jnp.bfloat16)
    y = x_ref[...] + jnp.dot(ag_ref[...], w_ref[0:A_WIDTH, :], preferred_element_type=jnp.float32)
    y_ref[...] = y + jnp.dot(b, w_ref[A_WIDTH:, :], preferred_element_type=jnp.float32)


def _sample_out(ot, zb, ag, x, w_out):
    return pl.pallas_call(
        _sample_out_kernel,
        out_shape=jax.ShapeDtypeStruct(x.shape, jnp.float32),
        name="sample_out",
    )(ot, zb, ag, x, w_out, jnp.eye(x.shape[0], dtype=jnp.bfloat16))


def kernel(x_prompt, x_sample, cache_k, cache_v, norm_g, w_in, sgu_g, w_spatial, b_spatial,
           q_norm_g, k_norm_g, w_out):
    depth = norm_g.shape[0]
    bp, tp, _ = x_prompt.shape
    bs, ts, _ = x_sample.shape
    assert bp == 1 and ts == 1 and cache_k.shape[2] == WINDOW and tp % (16 * BAND * ATTN_GROUP[16]) == 0
    keep = min(WINDOW, tp)
    xp = x_prompt.reshape(tp, D_MODEL)
    xs = x_sample.reshape(bs, D_MODEL)
    bd = (jnp.arange(256)[:, None] // HEAD_DIM == jnp.arange(256)[None, :] // HEAD_DIM).astype(jnp.bfloat16)
    ck = cache_k.transpose(0, 1, 3, 4, 2)
    cv = cache_v.transpose(0, 1, 3, 4, 2)
    kp_new, vp_new, ks_new, vs_new, sgu_new = [], [], [], [], []
    for l in range(depth):
        ng = norm_g[l][None, :]
        sg = sgu_g[l][None, :]
        qg = jnp.tile(q_norm_g[l], B_HEADS)[None, :]
        kg = jnp.tile(k_norm_g[l], B_HEADS)[None, :]
        w_in_l = w_in[l].astype(jnp.bfloat16)
        w_out_l = w_out[l].astype(jnp.bfloat16)
        wsp = w_spatial[l].transpose(1, 0, 2).reshape(CHUNK, A_GROUPS * CHUNK)
        bsp = jnp.repeat(b_spatial[l].T, HEAD_DIM, axis=1)

        nat, p4, p16, kf, vf, zb, ag = _prompt_proj(xp, ng, w_in_l, sg, wsp, bsp, qg, kg, bd, keep)
        a1 = _attn_pass(nat[None], 1)
        a4 = _attn_pass(p4, 4)
        a16 = _attn_pass(p16, 16)
        kp_new.append(kf.reshape(1, keep, B_HEADS, HEAD_DIM))
        vp_new.append(vf.reshape(1, keep, B_HEADS, HEAD_DIM))
        xp = _prompt_out(a1, a4, a16, zb, ag, xp, w_out_l)

        w0 = jnp.repeat(w_spatial[l, :, 0, 0], HEAD_DIM)[None, :]
        b0 = jnp.repeat(b_spatial[l, :, 0], HEAD_DIM)[None, :]
        qt, kt, vt, k, v, zbs, ags, vn = _sample_proj(xs, ng, w_in_l, sg, w0, b0, qg, kg, bd)
        heads = lambda a: a.reshape(B_HEADS, HEAD_DIM, bs)
        ot = _sample_attn(heads(qt), heads(kt), heads(vt), ck, cv, l)
        ks_new.append(k.reshape(bs, 1, B_HEADS, HEAD_DIM))
        vs_new.append(v.reshape(bs, 1, B_HEADS, HEAD_DIM))
        sgu_new.append(vn.reshape(bs, 1, A_WIDTH))
        xs = _sample_out(ot.reshape(B_WIDTH, bs), zbs, ags, xs, w_out_l)

    return (xp.reshape(bp, tp, D_MODEL), xs.reshape(bs, ts, D_MODEL),
            jnp.stack(kp_new, axis=0), jnp.stack(vp_new, axis=0),
            jnp.stack(ks_new, axis=0), jnp.stack(vs_new, axis=0), jnp.stack(sgu_new, axis=0))
```

```python
import functools
import math

import jax
import jax.numpy as jnp
from jax import lax
from jax.experimental import pallas as pl
from jax.experimental.pallas import tpu as pltpu

D_MODEL = 1024
HEAD_DIM = 64
A_WIDTH = 256
A_GROUPS = 4
B_WIDTH = 768
B_HEADS = 12
CHUNK = 128
PATTERNS = ((128, 1), (512, 4), (2048, 16))
WINDOW = 2048
PROJ_WIDTH = 3 * A_WIDTH + 4 * B_WIDTH
EPS = 1e-6
LOG2E = math.log2(math.e)

LANES = 128
BAND = 128
QKV_WIDTH = 3 * B_WIDTH
QKV_SLABS = QKV_WIDTH // LANES
O_SLABS = B_WIDTH // LANES
ACC_SLABS = O_SLABS + 1
NEG_BIG = -1e30
PROJ_TILE = 512
OUT_TILE = 1024
ATTN_GROUP = {1: 8, 4: 8, 16: 2}
VMEM_LIMIT = 56 * 1024 * 1024

_UA, _VA, _ZA = 0, A_WIDTH, 2 * A_WIDTH
_Q = 3 * A_WIDTH
_K = _Q + B_WIDTH
_V = _K + B_WIDTH
_ZB = _V + B_WIDTH


def _alibi_slopes():
    return 2.0 ** (-8.0 * jnp.arange(1, B_HEADS + 1, dtype=jnp.float32) / B_HEADS)


def _silu(z):
    return z * (1.0 / (1.0 + jnp.exp(-z)))


def _rms_scale(x):
    return lax.rsqrt(jnp.mean(x * x, axis=-1, keepdims=True) + EPS)


def _head_mean_square(t, bd):
    sq = (t * t).astype(jnp.bfloat16)
    parts = [jnp.dot(sq[:, j * 256:(j + 1) * 256], bd, preferred_element_type=jnp.float32)
             for j in range(B_WIDTH // 256)]
    return jnp.concatenate(parts, axis=1) * (1.0 / HEAD_DIM)


def _project(x, ng, w_ref, sg, qg, kg, bd, q_scale):
    h = (x * _rms_scale(x) * ng).astype(jnp.bfloat16)

    def proj(lo, width):
        return jnp.dot(h, w_ref[:, lo:lo + width], preferred_element_type=jnp.float32)

    ua = proj(_UA, A_WIDTH)
    va = proj(_VA, A_WIDTH)
    za = proj(_ZA, A_WIDTH)
    q = proj(_Q, B_WIDTH)
    k = proj(_K, B_WIDTH)
    v = proj(_V, B_WIDTH)
    zb = proj(_ZB, B_WIDTH)
    vn = va * _rms_scale(va) * sg
    qn = q * lax.rsqrt(_head_mean_square(q, bd) + EPS) * qg * q_scale
    kn = k * lax.rsqrt(_head_mean_square(k, bd) + EPS) * kg
    return ua, vn, za, qn, kn, v, zb


def _prompt_proj_kernel(x_ref, ng_ref, w_ref, sg_ref, wsp_ref, bsp_ref, qg_ref, kg_ref, bd_ref,
                        nat_ref, p4_ref, p16_ref, kf_ref, vf_ref, zb_ref, ag_ref,
                        wtril_scr, nat_scr, c4_scr):
    @pl.when(pl.program_id(0) == 0)
    def _():
        t = lax.broadcasted_iota(jnp.int32, (CHUNK, A_GROUPS * CHUNK), 0)
        s = lax.broadcasted_iota(jnp.int32, (CHUNK, A_GROUPS * CHUNK), 1) % CHUNK
        wtril_scr[...] = jnp.where(s <= t, wsp_ref[...], 0.0).astype(jnp.bfloat16)

    ua, vn, za, qn, kn, v, zb = _project(x_ref[...], ng_ref[...], w_ref, sg_ref[...], qg_ref[...],
                                         kg_ref[...], bd_ref[...], HEAD_DIM ** -0.5 * LOG2E)

    group = lax.broadcasted_iota(jnp.int32, (CHUNK, A_WIDTH), 1) // HEAD_DIM
    vnb = vn.astype(jnp.bfloat16)
    mixed = []
    for c in range(PROJ_TILE // CHUNK):
        vc = vnb[c * CHUNK:(c + 1) * CHUNK]
        rhs = jnp.concatenate([jnp.where(group == g, vc, jnp.zeros_like(vc)) for g in range(A_GROUPS)], axis=0)
        mixed.append(jnp.dot(wtril_scr[...], rhs, preferred_element_type=jnp.float32) + bsp_ref[...])
    mixed = jnp.concatenate(mixed, axis=0)
    ag_ref[...] = (ua * mixed * _silu(za)).astype(ag_ref.dtype)

    kf_ref[...] = kn
    vf_ref[...] = v
    zb_ref[...] = _silu(zb).astype(zb_ref.dtype)
    qkv = jnp.concatenate([qn, kn, v], axis=1)
    nat_ref[...] = qkv.astype(nat_ref.dtype)
    for s in range(QKV_SLABS):
        nat_scr[s] = qkv[:, s * LANES:(s + 1) * LANES]
    rows4 = PROJ_TILE // 4
    rows16 = PROJ_TILE // 16
    for s in range(QKV_SLABS):
        lanes = slice(s * LANES, (s + 1) * LANES)
        for r4 in range(4):
            t4 = nat_scr[s, pl.ds(r4, rows4, stride=4), :]
            p4_ref[r4, :, lanes] = t4.astype(p4_ref.dtype)
            c4_scr[s, r4 * rows4:(r4 + 1) * rows4, :] = t4
        for r4 in range(4):
            for q4 in range(4):
                t16 = c4_scr[s, pl.ds(r4 * rows4 + q4, rows16, stride=4), :]
                p16_ref[r4 + 4 * q4, :, lanes] = t16.astype(p16_ref.dtype)


def _prompt_proj(x, ng, w_in, sg, wsp, bsp, qg, kg, bd, keep):
    t = x.shape[0]
    n = t // PROJ_TILE
    keep_blocks = keep // PROJ_TILE
    const = lambda shape: pl.BlockSpec(shape, lambda i: (0,) * len(shape), pipeline_mode=pl.Buffered(1))
    row = lambda width: pl.BlockSpec((PROJ_TILE, width), lambda i: (i, 0))
    tail = pl.BlockSpec((PROJ_TILE, B_WIDTH), lambda i: (jnp.maximum(i - (n - keep_blocks), 0), 0))
    bf = jnp.bfloat16
    return pl.pallas_call(
        _prompt_proj_kernel,
        grid=(n,),
        in_specs=[row(D_MODEL), const((1, D_MODEL)), const((D_MODEL, PROJ_WIDTH)), const((1, A_WIDTH)),
                  const((CHUNK, A_GROUPS * CHUNK)), const((CHUNK, A_WIDTH)), const((1, B_WIDTH)),
                  const((1, B_WIDTH)), const((256, 256))],
        out_specs=[row(QKV_WIDTH),
                   pl.BlockSpec((4, PROJ_TILE // 4, QKV_WIDTH), lambda i: (0, i, 0)),
                   pl.BlockSpec((16, PROJ_TILE // 16, QKV_WIDTH), lambda i: (0, i, 0)),
                   tail, tail, row(B_WIDTH), row(A_WIDTH)],
        out_shape=[jax.ShapeDtypeStruct((t, QKV_WIDTH), bf),
                   jax.ShapeDtypeStruct((4, t // 4, QKV_WIDTH), bf),
                   jax.ShapeDtypeStruct((16, t // 16, QKV_WIDTH), bf),
                   jax.ShapeDtypeStruct((keep, B_WIDTH), jnp.float32),
                   jax.ShapeDtypeStruct((keep, B_WIDTH), jnp.float32),
                   jax.ShapeDtypeStruct((t, B_WIDTH), bf),
                   jax.ShapeDtypeStruct((t, A_WIDTH), bf)],
        scratch_shapes=[pltpu.VMEM((CHUNK, A_GROUPS * CHUNK), bf),
                        pltpu.VMEM((QKV_SLABS, PROJ_TILE, LANES), jnp.float32),
                        pltpu.VMEM((QKV_SLABS, PROJ_TILE, LANES), jnp.float32)],
        compiler_params=pltpu.CompilerParams(dimension_semantics=("arbitrary",), vmem_limit_bytes=VMEM_LIMIT),
        name="prompt_proj",
    )(x, ng, w_in, sg, wsp, bsp, qg, kg, bd)


def _band_bias(d):
    a = jnp.arange(BAND)[:, None]
    c = jnp.arange(2 * BAND)[None, :]
    delta = a - c + BAND
    valid = (delta >= 0) & (delta <= BAND)
    dist = (delta * d).astype(jnp.float32)
    pen = -(_alibi_slopes()[:, None, None] * dist[None]) * LOG2E
    general = jnp.where(valid[None], pen, NEG_BIG)
    first = jnp.where((c >= BAND)[None], general, NEG_BIG)
    return jnp.stack([first, general], axis=0)


def _attend_block(q, k, v, bias):
    low = lax.broadcasted_iota(jnp.int32, (1, LANES), 1) < HEAD_DIM
    lane = lax.broadcasted_iota(jnp.int32, (1, LANES), 1)
    lse = jnp.zeros((BAND, LANES), jnp.float32)
    tiles = []
    for hp in range(O_SLABS):
        qp = q[:, hp * LANES:(hp + 1) * LANES]
        kp = k[:, hp * LANES:(hp + 1) * LANES]
        vp = v[:, hp * LANES:(hp + 1) * LANES]
        halves = []
        for half in range(2):
            h = 2 * hp + half
            sel = low if half == 0 else jnp.logical_not(low)
            qh = jnp.where(sel, qp, jnp.zeros_like(qp))
            s = lax.dot_general(qh, kp, (((1,), (1,)), ((), ())), preferred_element_type=jnp.float32)
            s = s + bias(h)
            m = jnp.max(s, axis=-1, keepdims=True)
            p = jnp.exp2(s - m)
            l = jnp.sum(p, axis=-1, keepdims=True)
            o = jnp.dot(p.astype(jnp.bfloat16), vp, preferred_element_type=jnp.float32)
            halves.append(o * (1.0 / l))
            lse = jnp.where(lane == h, m + jnp.log2(l), lse)
        tiles.append(jnp.where(low, halves[0], halves[1]))
    tiles.append(lse)
    return tiles


def _attn_kernel(q_ref, kp_ref, kc_ref, vp_ref, vc_ref, bias_ref, *out_refs, d, group):
    i = pl.program_id(0)
    r = pl.program_id(1)
    first = jnp.minimum(i, 1)
    for g in range(group):
        q = q_ref[g * BAND:(g + 1) * BAND]
        if g == 0:
            k = jnp.concatenate([kp_ref[...], kc_ref[0:BAND]], axis=0)
            v = jnp.concatenate([vp_ref[...], vc_ref[0:BAND]], axis=0)
            bias = lambda h: bias_ref[first, h]
        else:
            k = kc_ref[(g - 1) * BAND:(g + 1) * BAND]
            v = vc_ref[(g - 1) * BAND:(g + 1) * BAND]
            bias = lambda h: bias_ref[1, h]
        tiles = _attend_block(q, k, v, bias)
        if d == 1:
            o_ref, lse_ref = out_refs
            for s in range(O_SLABS):
                o_ref[s, g * BAND:(g + 1) * BAND, :] = tiles[s].astype(o_ref.dtype)
            lse_ref[g * BAND:(g + 1) * BAND, :] = tiles[O_SLABS]
        else:
            out_ref, = out_refs
            for s, tile in enumerate(tiles):
                out_ref[s, pl.ds(r + g * BAND * d, BAND, stride=d), :] = tile


def _attn_pass(qkv_d, d):
    group = ATTN_GROUP[d]
    rows = group * BAND
    t = qkv_d.shape[0] * qkv_d.shape[1]
    steps = qkv_d.shape[1] // rows
    if d == 1:
        out_specs = [pl.BlockSpec((O_SLABS, rows, LANES), lambda i, r: (0, i, 0)),
                     pl.BlockSpec((rows, LANES), lambda i, r: (i, 0))]
        out_shape = [jax.ShapeDtypeStruct((O_SLABS, t, LANES), jnp.bfloat16),
                     jax.ShapeDtypeStruct((t, LANES), jnp.float32)]
    else:
        out_specs = pl.BlockSpec((ACC_SLABS, d * rows, LANES), lambda i, r: (0, i, 0))
        out_shape = jax.ShapeDtypeStruct((ACC_SLABS, t, LANES), jnp.float32)
    cur = lambda col: pl.BlockSpec((None, rows, B_WIDTH), lambda i, r: (r, i, col))
    prev = lambda col: pl.BlockSpec((None, BAND, B_WIDTH), lambda i, r: (r, jnp.maximum(group * i - 1, 0), col))
    return pl.pallas_call(
        functools.partial(_attn_kernel, d=d, group=group),
        grid=(steps, d),
        in_specs=[cur(0), prev(1), cur(1), prev(2), cur(2),
                  pl.BlockSpec((2, B_HEADS, BAND, 2 * BAND), lambda i, r: (0, 0, 0, 0),
                               pipeline_mode=pl.Buffered(1))],
        out_specs=out_specs,
        out_shape=out_shape,
        compiler_params=pltpu.CompilerParams(dimension_semantics=("arbitrary", "arbitrary"),
                                             vmem_limit_bytes=VMEM_LIMIT),
        name=f"prompt_attn_d{d}",
    )(qkv_d, qkv_d, qkv_d, qkv_d, qkv_d, _band_bias(d))


def _expand_heads(scale, sel):
    hi = scale.astype(jnp.bfloat16)
    lo = (scale - hi.astype(jnp.float32)).astype(jnp.bfloat16)
    return (jnp.dot(hi, sel, preferred_element_type=jnp.float32)
            + jnp.dot(lo, sel, preferred_element_type=jnp.float32))


def _unpack(ref):
    return jnp.concatenate([ref[s] for s in range(O_SLABS)], axis=1), ref[O_SLABS]


def _prompt_out_kernel(o1_ref, lse1_ref, a4_ref, a16_ref, gate_ref, ag_ref, x_ref, w_ref, sel_ref, y_ref):
    lane = lax.broadcasted_iota(jnp.int32, (1, LANES), 1)
    o1 = jnp.concatenate([o1_ref[s].astype(jnp.float32) for s in range(O_SLABS)], axis=1)
    parts = [(o1, lse1_ref[...])] + [_unpack(ref) for ref in (a4_ref, a16_ref)]
    m = functools.reduce(jnp.maximum, [lse for _, lse in parts])
    ws = [jnp.exp2(lse - m) for _, lse in parts]
    den = functools.reduce(lambda a, b: a + b, ws)
    o = None
    for w, (tiles, _) in zip(ws, parts):
        scale = jnp.where(lane < B_HEADS, w / den, 0.0)
        term = _expand_heads(scale, sel_ref[...]) * tiles
        o = term if o is None else o + term
    b = (o * gate_ref[...].astype(jnp.float32)).astype(jnp.bfloat16)
    y = x_ref[...] + jnp.dot(ag_ref[...], w_ref[0:A_WIDTH, :], preferred_element_type=jnp.float32)
    y_ref[...] = y + jnp.dot(b, w_ref[A_WIDTH:, :], preferred_element_type=jnp.float32)


def _head_select():
    h = jnp.arange(LANES)[:, None]
    c = jnp.arange(B_WIDTH)[None, :] // HEAD_DIM
    return (h == c).astype(jnp.bfloat16)


def _prompt_out(a1, a4, a16, zb, ag, x, w_out):
    t = x.shape[0]
    o1, lse1 = a1
    acc = pl.BlockSpec((ACC_SLABS, OUT_TILE, LANES), lambda i: (0, i, 0))
    row = lambda width: pl.BlockSpec((OUT_TILE, width), lambda i: (i, 0))
    const = lambda shape: pl.BlockSpec(shape, lambda i: (0, 0))
    return pl.pallas_call(
        _prompt_out_kernel,
        grid=(t // OUT_TILE,),
        in_specs=[pl.BlockSpec((O_SLABS, OUT_TILE, LANES), lambda i: (0, i, 0)), row(LANES), acc, acc,
                  row(B_WIDTH), row(A_WIDTH), row(D_MODEL), const((D_MODEL, D_MODEL)),
                  const((LANES, B_WIDTH))],
        out_specs=row(D_MODEL),
        out_shape=jax.ShapeDtypeStruct((t, D_MODEL), jnp.float32),
        compiler_params=pltpu.CompilerParams(dimension_semantics=("arbitrary",), vmem_limit_bytes=VMEM_LIMIT),
        name="prompt_out",
    )(o1, lse1, a4, a16, zb, ag, x, w_out, _head_select())


def _transpose_by_identity(eye, t):
    out = None
    rem = t
    for _ in range(3):
        piece = rem.astype(jnp.bfloat16)
        rem = rem - piece.astype(jnp.float32)
        term = lax.dot_general(eye, piece, (((1,), (1,)), ((), ())), preferred_element_type=jnp.float32)
        out = term if out is None else out + term
    return out


def _sample_proj_kernel(x_ref, ng_ref, w_ref, sg_ref, w0_ref, b0_ref, qg_ref, kg_ref, bd_ref, eye_ref,
                        qt_ref, kt_ref, vt_ref, k_ref, v_ref, zb_ref, ag_ref, vn_ref):
    ua, vn, za, qn, kn, v, zb = _project(x_ref[...], ng_ref[...], w_ref, sg_ref[...], qg_ref[...],
                                         kg_ref[...], bd_ref[...], HEAD_DIM ** -0.5)
    mixed = w0_ref[...] * vn + b0_ref[...]
    ag_ref[...] = (ua * mixed * _silu(za)).astype(ag_ref.dtype)
    qt_ref[...] = _transpose_by_identity(eye_ref[...], qn)
    kt_ref[...] = _transpose_by_identity(eye_ref[...], kn)
    vt_ref[...] = _transpose_by_identity(eye_ref[...], v)
    k_ref[...] = kn
    v_ref[...] = v
    zb_ref[...] = zb
    vn_ref[...] = vn


def _sample_proj(x, ng, w_in, sg, w0, b0, qg, kg, bd):
    n = x.shape[0]
    f32 = jnp.float32
    wide = jax.ShapeDtypeStruct((n, B_WIDTH), f32)
    tall = jax.ShapeDtypeStruct((B_WIDTH, n), f32)
    return pl.pallas_call(
        _sample_proj_kernel,
        out_shape=[tall, tall, tall, wide, wide, wide, jax.ShapeDtypeStruct((n, A_WIDTH), jnp.bfloat16),
                   jax.ShapeDtypeStruct((n, A_WIDTH), f32)],
        compiler_params=pltpu.CompilerParams(vmem_limit_bytes=VMEM_LIMIT),
        name="sample_proj",
    )(x, ng, w_in, sg, w0, b0, qg, kg, bd, jnp.eye(B_WIDTH, dtype=jnp.bfloat16))


def _window_bias():
    dist = WINDOW - jnp.arange(WINDOW)
    count = sum(((dist % d == 0) & (dist <= win)).astype(jnp.float32) for win, d in PATTERNS)
    pen = -(_alibi_slopes()[:, None] * dist.astype(jnp.float32)[None, :])
    bias = jnp.where((count > 0)[None, :], pen, NEG_BIG)
    return bias[:, None, :], count[None, None, :]


def _sample_attn_kernel(q_ref, kn_ref, vn_ref, kt_ref, vt_ref, bias_ref, count_ref, o_ref):
    b = pl.program_id(0)
    mine = lax.broadcasted_iota(jnp.int32, (1, 1, q_ref.shape[-1]), 2) == b
    pick = lambda ref: jnp.sum(jnp.where(mine, ref[...], 0.0), axis=-1, keepdims=True)
    q = pick(q_ref)
    k_new = pick(kn_ref)
    v_new = pick(vn_ref)
    s_new = jnp.sum(k_new * q, axis=1, keepdims=True)
    s = jnp.sum(kt_ref[...] * q, axis=1, keepdims=True) + bias_ref[...]
    m = jnp.maximum(jnp.max(s, axis=-1, keepdims=True), s_new)
    p = count_ref[...] * jnp.exp(s - m)
    p_new = float(len(PATTERNS)) * jnp.exp(s_new - m)
    den = jnp.sum(p, axis=-1, keepdims=True) + p_new
    num = jnp.sum(p * vt_ref[...], axis=-1, keepdims=True) + p_new * v_new

    @pl.when(b == 0)
    def _():
        o_ref[...] = jnp.zeros_like(o_ref)

    o_ref[...] = jnp.where(mine, num / den, o_ref[...])


def _sample_attn(q, kn, vn, ck, cv, layer):
    n = q.shape[-1]
    tok = pl.BlockSpec((B_HEADS, HEAD_DIM, n), lambda b: (0, 0, 0))
    win = pl.BlockSpec((None, None, B_HEADS, HEAD_DIM, WINDOW), lambda b: (layer, b, 0, 0, 0))
    bias, count = _window_bias()
    return pl.pallas_call(
        _sample_attn_kernel,
        grid=(n,),
        in_specs=[tok, tok, tok, win, win,
                  pl.BlockSpec((B_HEADS, 1, WINDOW), lambda b: (0, 0, 0)),
                  pl.BlockSpec((1, 1, WINDOW), lambda b: (0, 0, 0))],
        out_specs=tok,
        out_shape=jax.ShapeDtypeStruct((B_HEADS, HEAD_DIM, n), jnp.float32),
        compiler_params=pltpu.CompilerParams(dimension_semantics=("arbitrary",), vmem_limit_bytes=VMEM_LIMIT),
        name="sample_attn",
    )(q, kn, vn, ck, cv, bias, count)


def _sample_out_kernel(ot_ref, zb_ref, ag_ref, x_ref, w_ref, eye_ref, y_ref):
    o = _transpose_by_identity(eye_ref[...], ot_ref[...])
    b = (o * _silu(zb_ref[...])).astype(jnp.bfloat16)
    y = x_ref[...] + jnp.dot(ag_ref[...], w_ref[0:A_WIDTH, :], preferred_element_type=jnp.float32)
    y_ref[...] = y + jnp.dot(b, w_ref[A_WIDTH:, :], preferred_element_type=jnp.float32)


def _sample_out(ot, zb, ag, x, w_out):
    return pl.pallas_call(
        _sample_out_kernel,
        out_shape=jax.ShapeDtypeStruct(x.shape, jnp.float32),
        name="sample_out",
    )(ot, zb, ag, x, w_out, jnp.eye(x.shape[0], dtype=jnp.bfloat16))


def kernel(x_prompt, x_sample, cache_k, cache_v, norm_g, w_in, sgu_g, w_spatial, b_spatial,
           q_norm_g, k_norm_g, w_out):
    depth = norm_g.shape[0]
    bp, tp, _ = x_prompt.shape
    bs, ts, _ = x_sample.shape
    assert bp == 1 and ts == 1 and cache_k.shape[2] == WINDOW and tp % (16 * BAND * ATTN_GROUP[16]) == 0
    keep = min(WINDOW, tp)
    xp = x_prompt.reshape(tp, D_MODEL)
    xs = x_sample.reshape(bs, D_MODEL)
    bd = (jnp.arange(256)[:, None] // HEAD_DIM == jnp.arange(256)[None, :] // HEAD_DIM).astype(jnp.bfloat16)
    ck = cache_k.transpose(0, 1, 3, 4, 2)
    cv = cache_v.transpose(0, 1, 3, 4, 2)
    kp_new, vp_new, ks_new, vs_new, sgu_new = [], [], [], [], []
    for l in range(depth):
        ng = norm_g[l][None, :]
        sg = sgu_g[l][None, :]
        qg = jnp.tile(q_norm_g[l], B_HEADS)[None, :]
        kg = jnp.tile(k_norm_g[l], B_HEADS)[None, :]
        w_in_l = w_in[l].astype(jnp.bfloat16)
        w_out_l = w_out[l].astype(jnp.bfloat16)
        wsp = w_spatial[l].transpose(1, 0, 2).reshape(CHUNK, A_GROUPS * CHUNK)
        bsp = jnp.repeat(b_spatial[l].T, HEAD_DIM, axis=1)

        nat, p4, p16, kf, vf, zb, ag = _prompt_proj(xp, ng, w_in_l, sg, wsp, bsp, qg, kg, bd, keep)
        a1 = _attn_pass(nat[None], 1)
        a4 = _attn_pass(p4, 4)
        a16 = _attn_pass(p16, 16)
        kp_new.append(kf.reshape(1, keep, B_HEADS, HEAD_DIM))
        vp_new.append(vf.reshape(1, keep, B_HEADS, HEAD_DIM))
        xp = _prompt_out(a1, a4, a16, zb, ag, xp, w_out_l)

        w0 = jnp.repeat(w_spatial[l, :, 0, 0], HEAD_DIM)[None, :]
        b0 = jnp.repeat(b_spatial[l, :, 0], HEAD_DIM)[None, :]
        qt, kt, vt, k, v, zbs, ags, vn = _sample_proj(xs, ng, w_in_l, sg, w0, b0, qg, kg, bd)
        heads = lambda a: a.reshape(B_HEADS, HEAD_DIM, bs)
        ot = _sample_attn(heads(qt), heads(kt), heads(vt), ck, cv, l)
        ks_new.append(k.reshape(bs, 1, B_HEADS, HEAD_DIM))
        vs_new.append(v.reshape(bs, 1, B_HEADS, HEAD_DIM))
        sgu_new.append(vn.reshape(bs, 1, A_WIDTH))
        xs = _sample_out(ot.reshape(B_WIDTH, bs), zbs, ags, xs, w_out_l)

    return (xp.reshape(bp, tp, D_MODEL), xs.reshape(bs, ts, D_MODEL),
            jnp.stack(kp_new, axis=0), jnp.stack(vp_new, axis=0),
            jnp.stack(ks_new, axis=0), jnp.stack(vs_new, axis=0), jnp.stack(sgu_new, axis=0))
```

```python
import functools
import math

import jax
import jax.numpy as jnp
from jax import lax
from jax.experimental import pallas as pl
from jax.experimental.pallas import tpu as pltpu

D_MODEL = 1024
HEAD_DIM = 64
A_WIDTH = 256
A_GROUPS = 4
B_WIDTH = 768
B_HEADS = 12
CHUNK = 128
PATTERNS = ((128, 1), (512, 4), (2048, 16))
WINDOW = 2048
PROJ_WIDTH = 3 * A_WIDTH + 4 * B_WIDTH
EPS = 1e-6
LOG2E = math.log2(math.e)

LANES = 128
BAND = 128
QKV_WIDTH = 3 * B_WIDTH
QKV_SLABS = QKV_WIDTH // LANES
O_SLABS = B_WIDTH // LANES
ACC_SLABS = O_SLABS + 1
NEG_BIG = -1e30
PROJ_TILE = 512
OUT_TILE = 1024
ATTN_GROUP = {1: 8, 4: 8, 16: 4}
VMEM_LIMIT = 56 * 1024 * 1024

_UA, _VA, _ZA = 0, A_WIDTH, 2 * A_WIDTH
_Q = 3 * A_WIDTH
_K = _Q + B_WIDTH
_V = _K + B_WIDTH
_ZB = _V + B_WIDTH


def _alibi_slopes():
    return 2.0 ** (-8.0 * jnp.arange(1, B_HEADS + 1, dtype=jnp.float32) / B_HEADS)


def _silu(z):
    return z * (1.0 / (1.0 + jnp.exp(-z)))


def _rms_scale(x):
    return lax.rsqrt(jnp.mean(x * x, axis=-1, keepdims=True) + EPS)


def _head_mean_square(t, bd):
    sq = (t * t).astype(jnp.bfloat16)
    parts = [jnp.dot(sq[:, j * 256:(j + 1) * 256], bd, preferred_element_type=jnp.float32)
             for j in range(B_WIDTH // 256)]
    return jnp.concatenate(parts, axis=1) * (1.0 / HEAD_DIM)


def _project(x, ng, w_ref, sg, qg, kg, bd, q_scale):
    h = (x * _rms_scale(x) * ng).astype(jnp.bfloat16)

    def proj(lo, width):
        return jnp.dot(h, w_ref[:, lo:lo + width], preferred_element_type=jnp.float32)

    ua = proj(_UA, A_WIDTH)
    va = proj(_VA, A_WIDTH)
    za = proj(_ZA, A_WIDTH)
    q = proj(_Q, B_WIDTH)
    k = proj(_K, B_WIDTH)
    v = proj(_V, B_WIDTH)
    zb = proj(_ZB, B_WIDTH)
    vn = va * _rms_scale(va) * sg
    qn = q * lax.rsqrt(_head_mean_square(q, bd) + EPS) * qg * q_scale
    kn = k * lax.rsqrt(_head_mean_square(k, bd) + EPS) * kg
    return ua, vn, za, qn, kn, v, zb


def _prompt_proj_kernel(x_ref, ng_ref, w_ref, sg_ref, wsp_ref, bsp_ref, qg_ref, kg_ref, bd_ref,
                        nat_ref, p4_ref, p16_ref, kf_ref, vf_ref, zb_ref, ag_ref,
                        wtril_scr, nat_scr, c4_scr):
    @pl.when(pl.program_id(0) == 0)
    def _():
        t = lax.broadcasted_iota(jnp.int32, (CHUNK, A_GROUPS * CHUNK), 0)
        s = lax.broadcasted_iota(jnp.int32, (CHUNK, A_GROUPS * CHUNK), 1) % CHUNK
        wtril_scr[...] = jnp.where(s <= t, wsp_ref[...], 0.0).astype(jnp.bfloat16)

    ua, vn, za, qn, kn, v, zb = _project(x_ref[...], ng_ref[...], w_ref, sg_ref[...], qg_ref[...],
                                         kg_ref[...], bd_ref[...], HEAD_DIM ** -0.5 * LOG2E)

    group = lax.broadcasted_iota(jnp.int32, (CHUNK, A_WIDTH), 1) // HEAD_DIM
    vnb = vn.astype(jnp.bfloat16)
    mixed = []
    for c in range(PROJ_TILE // CHUNK):
        vc = vnb[c * CHUNK:(c + 1) * CHUNK]
        rhs = jnp.concatenate([jnp.where(group == g, vc, jnp.zeros_like(vc)) for g in range(A_GROUPS)], axis=0)
        mixed.append(jnp.dot(wtril_scr[...], rhs, preferred_element_type=jnp.float32) + bsp_ref[...])
    mixed = jnp.concatenate(mixed, axis=0)
    ag_ref[...] = (ua * mixed * _silu(za)).astype(ag_ref.dtype)

    kf_ref[...] = kn
    vf_ref[...] = v
    zb_ref[...] = _silu(zb).astype(zb_ref.dtype)
    qkv = jnp.concatenate([qn, kn, v], axis=1)
    nat_ref[...] = qkv.astype(nat_ref.dtype)
    for s in range(QKV_SLABS):
        nat_scr[s] = qkv[:, s * LANES:(s + 1) * LANES]
    rows4 = PROJ_TILE // 4
    rows16 = PROJ_TILE // 16
    for s in range(QKV_SLABS):
        lanes = slice(s * LANES, (s + 1) * LANES)
        for r4 in range(4):
            t4 = nat_scr[s, pl.ds(r4, rows4, stride=4), :]
            p4_ref[r4, :, lanes] = t4.astype(p4_ref.dtype)
            c4_scr[s, r4 * rows4:(r4 + 1) * rows4, :] = t4
        for r4 in range(4):
            for q4 in range(4):
                t16 = c4_scr[s, pl.ds(r4 * rows4 + q4, rows16, stride=4), :]
                p16_ref[r4 + 4 * q4, :, lanes] = t16.astype(p16_ref.dtype)


def _prompt_proj(x, ng, w_in, sg, wsp, bsp, qg, kg, bd, keep):
    t = x.shape[0]
    n = t // PROJ_TILE
    keep_blocks = keep // PROJ_TILE
    const = lambda shape: pl.BlockSpec(shape, lambda i: (0,) * len(shape), pipeline_mode=pl.Buffered(1))
    row = lambda width: pl.BlockSpec((PROJ_TILE, width), lambda i: (i, 0))
    tail = pl.BlockSpec((PROJ_TILE, B_WIDTH), lambda i: (jnp.maximum(i - (n - keep_blocks), 0), 0))
    bf = jnp.bfloat16
    return pl.pallas_call(
        _prompt_proj_kernel,
        grid=(n,),
        in_specs=[row(D_MODEL), const((1, D_MODEL)), const((D_MODEL, PROJ_WIDTH)), const((1, A_WIDTH)),
                  const((CHUNK, A_GROUPS * CHUNK)), const((CHUNK, A_WIDTH)), const((1, B_WIDTH)),
                  const((1, B_WIDTH)), const((256, 256))],
        out_specs=[row(QKV_WIDTH),
                   pl.BlockSpec((4, PROJ_TILE // 4, QKV_WIDTH), lambda i: (0, i, 0)),
                   pl.BlockSpec((16, PROJ_TILE // 16, QKV_WIDTH), lambda i: (0, i, 0)),
                   tail, tail, row(B_WIDTH), row(A_WIDTH)],
        out_shape=[jax.ShapeDtypeStruct((t, QKV_WIDTH), bf),
                   jax.ShapeDtypeStruct((4, t // 4, QKV_WIDTH), bf),
                   jax.ShapeDtypeStruct((16, t // 16, QKV_WIDTH), bf),
                   jax.ShapeDtypeStruct((keep, B_WIDTH), jnp.float32),
                   jax.ShapeDtypeStruct((keep, B_WIDTH), jnp.float32),
                   jax.ShapeDtypeStruct((t, B_WIDTH), bf),
                   jax.ShapeDtypeStruct((t, A_WIDTH), bf)],
        scratch_shapes=[pltpu.VMEM((CHUNK, A_GROUPS * CHUNK), bf),
                        pltpu.VMEM((QKV_SLABS, PROJ_TILE, LANES), jnp.float32),
                        pltpu.VMEM((QKV_SLABS, PROJ_TILE, LANES), jnp.float32)],
        compiler_params=pltpu.CompilerParams(dimension_semantics=("arbitrary",), vmem_limit_bytes=VMEM_LIMIT),
        name="prompt_proj",
    )(x, ng, w_in, sg, wsp, bsp, qg, kg, bd)


def _band_bias(d):
    a = jnp.arange(BAND)[:, None]
    c = jnp.arange(2 * BAND)[None, :]
    delta = a - c + BAND
    valid = (delta >= 0) & (delta <= BAND)
    dist = (delta * d).astype(jnp.float32)
    pen = -(_alibi_slopes()[:, None, None] * dist[None]) * LOG2E
    general = jnp.where(valid[None], pen, NEG_BIG)
    first = jnp.where((c >= BAND)[None], general, NEG_BIG)
    return jnp.stack([first, general], axis=0)


def _attend_block(q, k, v, bias):
    low = lax.broadcasted_iota(jnp.int32, (1, LANES), 1) < HEAD_DIM
    lane = lax.broadcasted_iota(jnp.int32, (1, LANES), 1)
    lse = jnp.zeros((BAND, LANES), jnp.float32)
    tiles = []
    for hp in range(O_SLABS):
        qp = q[:, hp * LANES:(hp + 1) * LANES]
        kp = k[:, hp * LANES:(hp + 1) * LANES]
        vp = v[:, hp * LANES:(hp + 1) * LANES]
        halves = []
        for half in range(2):
            h = 2 * hp + half
            sel = low if half == 0 else jnp.logical_not(low)
            qh = jnp.where(sel, qp, jnp.zeros_like(qp))
            s = lax.dot_general(qh, kp, (((1,), (1,)), ((), ())), preferred_element_type=jnp.float32)
            s = s + bias(h)
            m = jnp.max(s, axis=-1, keepdims=True)
            p = jnp.exp2(s - m)
            l = jnp.sum(p, axis=-1, keepdims=True)
            o = jnp.dot(p.astype(jnp.bfloat16), vp, preferred_element_type=jnp.float32)
            halves.append(o * (1.0 / l))
            lse = jnp.where(lane == h, m + jnp.log2(l), lse)
        tiles.append(jnp.where(low, halves[0], halves[1]))
    tiles.append(lse)
    return tiles


def _attn_kernel(q_ref, kp_ref, kc_ref, vp_ref, vc_ref, bias_ref, *out_refs, d, group):
    i = pl.program_id(0)
    r = pl.program_id(1)
    first = jnp.minimum(i, 1)
    for g in range(group):
        q = q_ref[g * BAND:(g + 1) * BAND]
        if g == 0:
            k = jnp.concatenate([kp_ref[...], kc_ref[0:BAND]], axis=0)
            v = jnp.concatenate([vp_ref[...], vc_ref[0:BAND]], axis=0)
            bias = lambda h: bias_ref[first, h]
        else:
            k = kc_ref[(g - 1) * BAND:(g + 1) * BAND]
            v = vc_ref[(g - 1) * BAND:(g + 1) * BAND]
            bias = lambda h: bias_ref[1, h]
        tiles = _attend_block(q, k, v, bias)
        if d == 1:
            o_ref, lse_ref = out_refs
            for s in range(O_SLABS):
                o_ref[s, g * BAND:(g + 1) * BAND, :] = tiles[s].astype(o_ref.dtype)
            lse_ref[g * BAND:(g + 1) * BAND, :] = tiles[O_SLABS]
        else:
            out_ref, = out_refs
            for s, tile in enumerate(tiles):
                out_ref[s, pl.ds(r + g * BAND * d, BAND, stride=d), :] = tile


def _attn_pass(qkv_d, d):
    group = ATTN_GROUP[d]
    rows = group * BAND
    t = qkv_d.shape[0] * qkv_d.shape[1]
    steps = qkv_d.shape[1] // rows
    if d == 1:
        out_specs = [pl.BlockSpec((O_SLABS, rows, LANES), lambda i, r: (0, i, 0)),
                     pl.BlockSpec((rows, LANES), lambda i, r: (i, 0))]
        out_shape = [jax.ShapeDtypeStruct((O_SLABS, t, LANES), jnp.bfloat16),
                     jax.ShapeDtypeStruct((t, LANES), jnp.float32)]
    else:
        mode = dict(pipeline_mode=pl.Buffered(1)) if ACC_SLABS * d * rows * LANES * 4 > VMEM_LIMIT // 4 else {}
        out_specs = pl.BlockSpec((ACC_SLABS, d * rows, LANES), lambda i, r: (0, i, 0), **mode)
        out_shape = jax.ShapeDtypeStruct((ACC_SLABS, t, LANES), jnp.float32)
    cur = lambda col: pl.BlockSpec((None, rows, B_WIDTH), lambda i, r: (r, i, col))
    prev = lambda col: pl.BlockSpec((None, BAND, B_WIDTH), lambda i, r: (r, jnp.maximum(group * i - 1, 0), col))
    return pl.pallas_call(
        functools.partial(_attn_kernel, d=d, group=group),
        grid=(steps, d),
        in_specs=[cur(0), prev(1), cur(1), prev(2), cur(2),
                  pl.BlockSpec((2, B_HEADS, BAND, 2 * BAND), lambda i, r: (0, 0, 0, 0),
                               pipeline_mode=pl.Buffered(1))],
        out_specs=out_specs,
        out_shape=out_shape,
        compiler_params=pltpu.CompilerParams(dimension_semantics=("arbitrary", "arbitrary"),
                                             vmem_limit_bytes=VMEM_LIMIT),
        name=f"prompt_attn_d{d}",
    )(qkv_d, qkv_d, qkv_d, qkv_d, qkv_d, _band_bias(d))


def _expand_heads(scale, sel):
    hi = scale.astype(jnp.bfloat16)
    lo = (scale - hi.astype(jnp.float32)).astype(jnp.bfloat16)
    return (jnp.dot(hi, sel, preferred_element_type=jnp.float32)
            + jnp.dot(lo, sel, preferred_element_type=jnp.float32))


def _unpack(ref):
    return jnp.concatenate([ref[s] for s in range(O_SLABS)], axis=1), ref[O_SLABS]


def _prompt_out_kernel(o1_ref, lse1_ref, a4_ref, a16_ref, gate_ref, ag_ref, x_ref, w_ref, sel_ref, y_ref):
    lane = lax.broadcasted_iota(jnp.int32, (1, LANES), 1)
    o1 = jnp.concatenate([o1_ref[s].astype(jnp.float32) for s in range(O_SLABS)], axis=1)
    parts = [(o1, lse1_ref[...])] + [_unpack(ref) for ref in (a4_ref, a16_ref)]
    m = functools.reduce(jnp.maximum, [lse for _, lse in parts])
    ws = [jnp.exp2(lse - m) for _, lse in parts]
    den = functools.reduce(lambda a, b: a + b, ws)
    o = None
    for w, (tiles, _) in zip(ws, parts):
        scale = jnp.where(lane < B_HEADS, w / den, 0.0)
        term = _expand_heads(scale, sel_ref[...]) * tiles
        o = term if o is None else o + term
    b = (o * gate_ref[...].astype(jnp.float32)).astype(jnp.bfloat16)
    y = x_ref[...] + jnp.dot(ag_ref[...], w_ref[0:A_WIDTH, :], preferred_element_type=jnp.float32)
    y_ref[...] = y + jnp.dot(b, w_ref[A_WIDTH:, :], preferred_element_type=jnp.float32)


def _head_select():
    h = jnp.arange(LANES)[:, None]
    c = jnp.arange(B_WIDTH)[None, :] // HEAD_DIM
    return (h == c).astype(jnp.bfloat16)


def _prompt_out(a1, a4, a16, zb, ag, x, w_out):
    t = x.shape[0]
    o1, lse1 = a1
    acc = pl.BlockSpec((ACC_SLABS, OUT_TILE, LANES), lambda i: (0, i, 0))
    row = lambda width: pl.BlockSpec((OUT_TILE, width), lambda i: (i, 0))
    const = lambda shape: pl.BlockSpec(shape, lambda i: (0, 0))
    return pl.pallas_call(
        _prompt_out_kernel,
        grid=(t // OUT_TILE,),
        in_specs=[pl.BlockSpec((O_SLABS, OUT_TILE, LANES), lambda i: (0, i, 0)), row(LANES), acc, acc,
                  row(B_WIDTH), row(A_WIDTH), row(D_MODEL), const((D_MODEL, D_MODEL)),
                  const((LANES, B_WIDTH))],
        out_specs=row(D_MODEL),
        out_shape=jax.ShapeDtypeStruct((t, D_MODEL), jnp.float32),
        compiler_params=pltpu.CompilerParams(dimension_semantics=("arbitrary",), vmem_limit_bytes=VMEM_LIMIT),
        name="prompt_out",
    )(o1, lse1, a4, a16, zb, ag, x, w_out, _head_select())


def _transpose_by_identity(eye, t):
    out = None
    rem = t
    for _ in range(3):
        piece = rem.astype(jnp.bfloat16)
        rem = rem - piece.astype(jnp.float32)
        term = lax.dot_general(eye, piece, (((1,), (1,)), ((), ())), preferred_element_type=jnp.float32)
        out = term if out is None else out + term
    return out


def _sample_proj_kernel(x_ref, ng_ref, w_ref, sg_ref, w0_ref, b0_ref, qg_ref, kg_ref, bd_ref, eye_ref,
                        qt_ref, kt_ref, vt_ref, k_ref, v_ref, zb_ref, ag_ref, vn_ref):
    ua, vn, za, qn, kn, v, zb = _project(x_ref[...], ng_ref[...], w_ref, sg_ref[...], qg_ref[...],
                                         kg_ref[...], bd_ref[...], HEAD_DIM ** -0.5)
    mixed = w0_ref[...] * vn + b0_ref[...]
    ag_ref[...] = (ua * mixed * _silu(za)).astype(ag_ref.dtype)
    qt_ref[...] = _transpose_by_identity(eye_ref[...], qn)
    kt_ref[...] = _transpose_by_identity(eye_ref[...], kn)
    vt_ref[...] = _transpose_by_identity(eye_ref[...], v)
    k_ref[...] = kn
    v_ref[...] = v
    zb_ref[...] = zb
    vn_ref[...] = vn


def _sample_proj(x, ng, w_in, sg, w0, b0, qg, kg, bd):
    n = x.shape[0]
    f32 = jnp.float32
    wide = jax.ShapeDtypeStruct((n, B_WIDTH), f32)
    tall = jax.ShapeDtypeStruct((B_WIDTH, n), f32)
    return pl.pallas_call(
        _sample_proj_kernel,
        out_shape=[tall, tall, tall, wide, wide, wide, jax.ShapeDtypeStruct((n, A_WIDTH), jnp.bfloat16),
                   jax.ShapeDtypeStruct((n, A_WIDTH), f32)],
        compiler_params=pltpu.CompilerParams(vmem_limit_bytes=VMEM_LIMIT),
        name="sample_proj",
    )(x, ng, w_in, sg, w0, b0, qg, kg, bd, jnp.eye(B_WIDTH, dtype=jnp.bfloat16))


def _window_bias():
    dist = WINDOW - jnp.arange(WINDOW)
    count = sum(((dist % d == 0) & (dist <= win)).astype(jnp.float32) for win, d in PATTERNS)
    pen = -(_alibi_slopes()[:, None] * dist.astype(jnp.float32)[None, :])
    bias = jnp.where((count > 0)[None, :], pen, NEG_BIG)
    return bias[:, None, :], count[None, None, :]


def _sample_attn_kernel(q_ref, kn_ref, vn_ref, kt_ref, vt_ref, bias_ref, count_ref, o_ref):
    b = pl.program_id(0)
    mine = lax.broadcasted_iota(jnp.int32, (1, 1, q_ref.shape[-1]), 2) == b
    pick = lambda ref: jnp.sum(jnp.where(mine, ref[...], 0.0), axis=-1, keepdims=True)
    q = pick(q_ref)
    k_new = pick(kn_ref)
    v_new = pick(vn_ref)
    s_new = jnp.sum(k_new * q, axis=1, keepdims=True)
    s = jnp.sum(kt_ref[...] * q, axis=1, keepdims=True) + bias_ref[...]
    m = jnp.maximum(jnp.max(s, axis=-1, keepdims=True), s_new)
    p = count_ref[...] * jnp.exp(s - m)
    p_new = float(len(PATTERNS)) * jnp.exp(s_new - m)
    den = jnp.sum(p, axis=-1, keepdims=True) + p_new
    num = jnp.sum(p * vt_ref[...], axis=-1, keepdims=True) + p_new * v_new

    @pl.when(b == 0)
    def _():
        o_ref[...] = jnp.zeros_like(o_ref)

    o_ref[...] = jnp.where(mine, num / den, o_ref[...])


def _sample_attn(q, kn, vn, ck, cv, layer):
    n = q.shape[-1]
    tok = pl.BlockSpec((B_HEADS, HEAD_DIM, n), lambda b: (0, 0, 0))
    win = pl.BlockSpec((None, None, B_HEADS, HEAD_DIM, WINDOW), lambda b: (layer, b, 0, 0, 0))
    bias, count = _window_bias()
    return pl.pallas_call(
        _sample_attn_kernel,
        grid=(n,),
        in_specs=[tok, tok, tok, win, win,
                  pl.BlockSpec((B_HEADS, 1, WINDOW), lambda b: (0, 0, 0)),
                  pl.BlockSpec((1, 1, WINDOW), lambda b: (0, 0, 0))],
        out_specs=tok,
        out_shape=jax.ShapeDtypeStruct((B_HEADS, HEAD_DIM, n), jnp.float32),
        compiler_params=pltpu.CompilerParams(dimension_semantics=("arbitrary",), vmem_limit_bytes=VMEM_LIMIT),
        name="sample_attn",
    )(q, kn, vn, ck, cv, bias, count)


def _sample_out_kernel(ot_ref, zb_ref, ag_ref, x_ref, w_ref, eye_ref, y_ref):
    o = _transpose_by_identity(eye_ref[...], ot_ref[...])
    b = (o * _silu(zb_ref[...])).astype(jnp.bfloat16)
    y = x_ref[...] + jnp.dot(ag_ref[...], w_ref[0:A_WIDTH, :], preferred_element_type=jnp.float32)
    y_ref[...] = y + jnp.dot(b, w_ref[A_WIDTH:, :], preferred_element_type=jnp.float32)


def _sample_out(ot, zb, ag, x, w_out):
    return pl.pallas_call(
        _sample_out_kernel,
        out_shape=jax.ShapeDtypeStruct(x.shape, jnp.float32),
        name="sample_out",
    )(ot, zb, ag, x, w_out, jnp.eye(x.shape[0], dtype=jnp.bfloat16))


def kernel(x_prompt, x_sample, cache_k, cache_v, norm_g, w_in, sgu_g, w_spatial, b_spatial,
           q_norm_g, k_norm_g, w_out):
    depth = norm_g.shape[0]
    bp, tp, _ = x_prompt.shape
    bs, ts, _ = x_sample.shape
    assert bp == 1 and ts == 1 and cache_k.shape[2] == WINDOW and tp % (16 * BAND * ATTN_GROUP[16]) == 0
    keep = min(WINDOW, tp)
    xp = x_prompt.reshape(tp, D_MODEL)
    xs = x_sample.reshape(bs, D_MODEL)
    bd = (jnp.arange(256)[:, None] // HEAD_DIM == jnp.arange(256)[None, :] // HEAD_DIM).astype(jnp.bfloat16)
    ck = cache_k.transpose(0, 1, 3, 4, 2)
    cv = cache_v.transpose(0, 1, 3, 4, 2)
    kp_new, vp_new, ks_new, vs_new, sgu_new = [], [], [], [], []
    for l in range(depth):
        ng = norm_g[l][None, :]
        sg = sgu_g[l][None, :]
        qg = jnp.tile(q_norm_g[l], B_HEADS)[None, :]
        kg = jnp.tile(k_norm_g[l], B_HEADS)[None, :]
        w_in_l = w_in[l].astype(jnp.bfloat16)
        w_out_l = w_out[l].astype(jnp.bfloat16)
        wsp = w_spatial[l].transpose(1, 0, 2).reshape(CHUNK, A_GROUPS * CHUNK)
        bsp = jnp.repeat(b_spatial[l].T, HEAD_DIM, axis=1)

        nat, p4, p16, kf, vf, zb, ag = _prompt_proj(xp, ng, w_in_l, sg, wsp, bsp, qg, kg, bd, keep)
        a1 = _attn_pass(nat[None], 1)
        a4 = _attn_pass(p4, 4)
        a16 = _attn_pass(p16, 16)
        kp_new.append(kf.reshape(1, keep, B_HEADS, HEAD_DIM))
        vp_new.append(vf.reshape(1, keep, B_HEADS, HEAD_DIM))
        xp = _prompt_out(a1, a4, a16, zb, ag, xp, w_out_l)

        w0 = jnp.repeat(w_spatial[l, :, 0, 0], HEAD_DIM)[None, :]
        b0 = jnp.repeat(b_spatial[l, :, 0], HEAD_DIM)[None, :]
        qt, kt, vt, k, v, zbs, ags, vn = _sample_proj(xs, ng, w_in_l, sg, w0, b0, qg, kg, bd)
        heads = lambda a: a.reshape(B_HEADS, HEAD_DIM, bs)
        ot = _sample_attn(heads(qt), heads(kt), heads(vt), ck, cv, l)
        ks_new.append(k.reshape(bs, 1, B_HEADS, HEAD_DIM))
        vs_new.append(v.reshape(bs, 1, B_HEADS, HEAD_DIM))
        sgu_new.append(vn.reshape(bs, 1, A_WIDTH))
        xs = _sample_out(ot.reshape(B_WIDTH, bs), zbs, ags, xs, w_out_l)

    return (xp.reshape(bp, tp, D_MODEL), xs.reshape(bs, ts, D_MODEL),
            jnp.stack(kp_new, axis=0), jnp.stack(vp_new, axis=0),
            jnp.stack(ks_new, axis=0), jnp.stack(vs_new, axis=0), jnp.stack(sgu_new, axis=0))
```

```python
import functools
import math

import jax
import jax.numpy as jnp
from jax import lax
from jax.experimental import pallas as pl
from jax.experimental.pallas import tpu as pltpu

D_MODEL = 1024
HEAD_DIM = 64
A_WIDTH = 256
A_GROUPS = 4
B_WIDTH = 768
B_HEADS = 12
CHUNK = 128
PATTERNS = ((128, 1), (512, 4), (2048, 16))
WINDOW = 2048
PROJ_WIDTH = 3 * A_WIDTH + 4 * B_WIDTH
EPS = 1e-6
LOG2E = math.log2(math.e)

LANES = 128
BAND = 128
QKV_WIDTH = 3 * B_WIDTH
QKV_SLABS = QKV_WIDTH // LANES
O_SLABS = B_WIDTH // LANES
ACC_SLABS = O_SLABS + 1
NEG_BIG = -1e30
PROJ_TILE = 512
OUT_TILE = 1024
ATTN_GROUP = {1: 8, 4: 8, 16: 8}
CLASS_ORDER_OUT = 16
VMEM_LIMIT = 56 * 1024 * 1024

_UA, _VA, _ZA = 0, A_WIDTH, 2 * A_WIDTH
_Q = 3 * A_WIDTH
_K = _Q + B_WIDTH
_V = _K + B_WIDTH
_ZB = _V + B_WIDTH


def _alibi_slopes():
    return 2.0 ** (-8.0 * jnp.arange(1, B_HEADS + 1, dtype=jnp.float32) / B_HEADS)


def _silu(z):
    return z * (1.0 / (1.0 + jnp.exp(-z)))


def _rms_scale(x):
    return lax.rsqrt(jnp.mean(x * x, axis=-1, keepdims=True) + EPS)


def _head_mean_square(t, bd):
    sq = (t * t).astype(jnp.bfloat16)
    parts = [jnp.dot(sq[:, j * 256:(j + 1) * 256], bd, preferred_element_type=jnp.float32)
             for j in range(B_WIDTH // 256)]
    return jnp.concatenate(parts, axis=1) * (1.0 / HEAD_DIM)


def _project(x, ng, w_ref, sg, qg, kg, bd, q_scale):
    h = (x * _rms_scale(x) * ng).astype(jnp.bfloat16)

    def proj(lo, width):
        return jnp.dot(h, w_ref[:, lo:lo + width], preferred_element_type=jnp.float32)

    ua = proj(_UA, A_WIDTH)
    va = proj(_VA, A_WIDTH)
    za = proj(_ZA, A_WIDTH)
    q = proj(_Q, B_WIDTH)
    k = proj(_K, B_WIDTH)
    v = proj(_V, B_WIDTH)
    zb = proj(_ZB, B_WIDTH)
    vn = va * _rms_scale(va) * sg
    qn = q * lax.rsqrt(_head_mean_square(q, bd) + EPS) * qg * q_scale
    kn = k * lax.rsqrt(_head_mean_square(k, bd) + EPS) * kg
    return ua, vn, za, qn, kn, v, zb


def _prompt_proj_kernel(x_ref, ng_ref, w_ref, sg_ref, wsp_ref, bsp_ref, qg_ref, kg_ref, bd_ref,
                        nat_ref, p4_ref, p16_ref, kf_ref, vf_ref, zb_ref, ag_ref,
                        wtril_scr, nat_scr, c4_scr):
    @pl.when(pl.program_id(0) == 0)
    def _():
        t = lax.broadcasted_iota(jnp.int32, (CHUNK, A_GROUPS * CHUNK), 0)
        s = lax.broadcasted_iota(jnp.int32, (CHUNK, A_GROUPS * CHUNK), 1) % CHUNK
        wtril_scr[...] = jnp.where(s <= t, wsp_ref[...], 0.0).astype(jnp.bfloat16)

    ua, vn, za, qn, kn, v, zb = _project(x_ref[...], ng_ref[...], w_ref, sg_ref[...], qg_ref[...],
                                         kg_ref[...], bd_ref[...], HEAD_DIM ** -0.5 * LOG2E)

    group = lax.broadcasted_iota(jnp.int32, (CHUNK, A_WIDTH), 1) // HEAD_DIM
    vnb = vn.astype(jnp.bfloat16)
    mixed = []
    for c in range(PROJ_TILE // CHUNK):
        vc = vnb[c * CHUNK:(c + 1) * CHUNK]
        rhs = jnp.concatenate([jnp.where(group == g, vc, jnp.zeros_like(vc)) for g in range(A_GROUPS)], axis=0)
        mixed.append(jnp.dot(wtril_scr[...], rhs, preferred_element_type=jnp.float32) + bsp_ref[...])
    mixed = jnp.concatenate(mixed, axis=0)
    ag_ref[...] = (ua * mixed * _silu(za)).astype(ag_ref.dtype)

    kf_ref[...] = kn
    vf_ref[...] = v
    zb_ref[...] = _silu(zb).astype(zb_ref.dtype)
    qkv = jnp.concatenate([qn, kn, v], axis=1)
    nat_ref[...] = qkv.astype(nat_ref.dtype)
    for s in range(QKV_SLABS):
        nat_scr[s] = qkv[:, s * LANES:(s + 1) * LANES]
    rows4 = PROJ_TILE // 4
    rows16 = PROJ_TILE // 16
    for s in range(QKV_SLABS):
        lanes = slice(s * LANES, (s + 1) * LANES)
        for r4 in range(4):
            t4 = nat_scr[s, pl.ds(r4, rows4, stride=4), :]
            p4_ref[r4, :, lanes] = t4.astype(p4_ref.dtype)
            c4_scr[s, r4 * rows4:(r4 + 1) * rows4, :] = t4
        for r4 in range(4):
            for q4 in range(4):
                t16 = c4_scr[s, pl.ds(r4 * rows4 + q4, rows16, stride=4), :]
                p16_ref[r4 + 4 * q4, :, lanes] = t16.astype(p16_ref.dtype)


def _prompt_proj(x, ng, w_in, sg, wsp, bsp, qg, kg, bd, keep):
    t = x.shape[0]
    n = t // PROJ_TILE
    keep_blocks = keep // PROJ_TILE
    const = lambda shape: pl.BlockSpec(shape, lambda i: (0,) * len(shape), pipeline_mode=pl.Buffered(1))
    row = lambda width: pl.BlockSpec((PROJ_TILE, width), lambda i: (i, 0))
    tail = pl.BlockSpec((PROJ_TILE, B_WIDTH), lambda i: (jnp.maximum(i - (n - keep_blocks), 0), 0))
    bf = jnp.bfloat16
    return pl.pallas_call(
        _prompt_proj_kernel,
        grid=(n,),
        in_specs=[row(D_MODEL), const((1, D_MODEL)), const((D_MODEL, PROJ_WIDTH)), const((1, A_WIDTH)),
                  const((CHUNK, A_GROUPS * CHUNK)), const((CHUNK, A_WIDTH)), const((1, B_WIDTH)),
                  const((1, B_WIDTH)), const((256, 256))],
        out_specs=[row(QKV_WIDTH),
                   pl.BlockSpec((4, PROJ_TILE // 4, QKV_WIDTH), lambda i: (0, i, 0)),
                   pl.BlockSpec((16, PROJ_TILE // 16, QKV_WIDTH), lambda i: (0, i, 0)),
                   tail, tail, row(B_WIDTH), row(A_WIDTH)],
        out_shape=[jax.ShapeDtypeStruct((t, QKV_WIDTH), bf),
                   jax.ShapeDtypeStruct((4, t // 4, QKV_WIDTH), bf),
                   jax.ShapeDtypeStruct((16, t // 16, QKV_WIDTH), bf),
                   jax.ShapeDtypeStruct((keep, B_WIDTH), jnp.float32),
                   jax.ShapeDtypeStruct((keep, B_WIDTH), jnp.float32),
                   jax.ShapeDtypeStruct((t, B_WIDTH), bf),
                   jax.ShapeDtypeStruct((t, A_WIDTH), bf)],
        scratch_shapes=[pltpu.VMEM((CHUNK, A_GROUPS * CHUNK), bf),
                        pltpu.VMEM((QKV_SLABS, PROJ_TILE, LANES), jnp.float32),
                        pltpu.VMEM((QKV_SLABS, PROJ_TILE, LANES), jnp.float32)],
        compiler_params=pltpu.CompilerParams(dimension_semantics=("arbitrary",), vmem_limit_bytes=VMEM_LIMIT),
        name="prompt_proj",
    )(x, ng, w_in, sg, wsp, bsp, qg, kg, bd)


def _band_bias(d):
    a = jnp.arange(BAND)[:, None]
    c = jnp.arange(2 * BAND)[None, :]
    delta = a - c + BAND
    valid = (delta >= 0) & (delta <= BAND)
    dist = (delta * d).astype(jnp.float32)
    pen = -(_alibi_slopes()[:, None, None] * dist[None]) * LOG2E
    general = jnp.where(valid[None], pen, NEG_BIG)
    first = jnp.where((c >= BAND)[None], general, NEG_BIG)
    return jnp.stack([first, general], axis=0)


def _attend_block(q, k, v, bias):
    low = lax.broadcasted_iota(jnp.int32, (1, LANES), 1) < HEAD_DIM
    lane = lax.broadcasted_iota(jnp.int32, (1, LANES), 1)
    lse = jnp.zeros((BAND, LANES), jnp.float32)
    tiles = []
    for hp in range(O_SLABS):
        qp = q[:, hp * LANES:(hp + 1) * LANES]
        kp = k[:, hp * LANES:(hp + 1) * LANES]
        vp = v[:, hp * LANES:(hp + 1) * LANES]
        halves = []
        for half in range(2):
            h = 2 * hp + half
            sel = low if half == 0 else jnp.logical_not(low)
            qh = jnp.where(sel, qp, jnp.zeros_like(qp))
            s = lax.dot_general(qh, kp, (((1,), (1,)), ((), ())), preferred_element_type=jnp.float32)
            s = s + bias(h)
            m = jnp.max(s, axis=-1, keepdims=True)
            p = jnp.exp2(s - m)
            l = jnp.sum(p, axis=-1, keepdims=True)
            o = jnp.dot(p.astype(jnp.bfloat16), vp, preferred_element_type=jnp.float32)
            halves.append(o * (1.0 / l))
            lse = jnp.where(lane == h, m + jnp.log2(l), lse)
        tiles.append(jnp.where(low, halves[0], halves[1]))
    tiles.append(lse)
    return tiles


def _attn_kernel(q_ref, kp_ref, kc_ref, vp_ref, vc_ref, bias_ref, *out_refs, d, group):
    i = pl.program_id(0)
    r = pl.program_id(1)
    first = jnp.minimum(i, 1)
    for g in range(group):
        q = q_ref[g * BAND:(g + 1) * BAND]
        if g == 0:
            k = jnp.concatenate([kp_ref[...], kc_ref[0:BAND]], axis=0)
            v = jnp.concatenate([vp_ref[...], vc_ref[0:BAND]], axis=0)
            bias = lambda h: bias_ref[first, h]
        else:
            k = kc_ref[(g - 1) * BAND:(g + 1) * BAND]
            v = vc_ref[(g - 1) * BAND:(g + 1) * BAND]
            bias = lambda h: bias_ref[1, h]
        tiles = _attend_block(q, k, v, bias)
        if d == 1:
            o_ref, lse_ref = out_refs
            for s in range(O_SLABS):
                o_ref[s, g * BAND:(g + 1) * BAND, :] = tiles[s].astype(o_ref.dtype)
            lse_ref[g * BAND:(g + 1) * BAND, :] = tiles[O_SLABS]
        elif d == CLASS_ORDER_OUT:
            out_ref, = out_refs
            for s, tile in enumerate(tiles):
                out_ref[s, g * BAND:(g + 1) * BAND, :] = tile
        else:
            out_ref, = out_refs
            for s, tile in enumerate(tiles):
                out_ref[s, pl.ds(r + g * BAND * d, BAND, stride=d), :] = tile


def _attn_pass(qkv_d, d):
    group = ATTN_GROUP[d]
    rows = group * BAND
    t = qkv_d.shape[0] * qkv_d.shape[1]
    steps = qkv_d.shape[1] // rows
    if d == 1:
        out_specs = [pl.BlockSpec((O_SLABS, rows, LANES), lambda i, r: (0, i, 0)),
                     pl.BlockSpec((rows, LANES), lambda i, r: (i, 0))]
        out_shape = [jax.ShapeDtypeStruct((O_SLABS, t, LANES), jnp.bfloat16),
                     jax.ShapeDtypeStruct((t, LANES), jnp.float32)]
    elif d == CLASS_ORDER_OUT:
        out_specs = pl.BlockSpec((ACC_SLABS, None, rows, LANES), lambda i, r: (0, r, i, 0))
        out_shape = jax.ShapeDtypeStruct((ACC_SLABS, d, t // d, LANES), jnp.float32)
    else:
        mode = dict(pipeline_mode=pl.Buffered(1)) if ACC_SLABS * d * rows * LANES * 4 > VMEM_LIMIT // 4 else {}
        out_specs = pl.BlockSpec((ACC_SLABS, d * rows, LANES), lambda i, r: (0, i, 0), **mode)
        out_shape = jax.ShapeDtypeStruct((ACC_SLABS, t, LANES), jnp.float32)
    cur = lambda col: pl.BlockSpec((None, rows, B_WIDTH), lambda i, r: (r, i, col))
    prev = lambda col: pl.BlockSpec((None, BAND, B_WIDTH), lambda i, r: (r, jnp.maximum(group * i - 1, 0), col))
    return pl.pallas_call(
        functools.partial(_attn_kernel, d=d, group=group),
        grid=(steps, d),
        in_specs=[cur(0), prev(1), cur(1), prev(2), cur(2),
                  pl.BlockSpec((2, B_HEADS, BAND, 2 * BAND), lambda i, r: (0, 0, 0, 0),
                               pipeline_mode=pl.Buffered(1))],
        out_specs=out_specs,
        out_shape=out_shape,
        compiler_params=pltpu.CompilerParams(dimension_semantics=("arbitrary", "arbitrary"),
                                             vmem_limit_bytes=VMEM_LIMIT),
        name=f"prompt_attn_d{d}",
    )(qkv_d, qkv_d, qkv_d, qkv_d, qkv_d, _band_bias(d))


def _expand_heads(scale, sel):
    hi = scale.astype(jnp.bfloat16)
    lo = (scale - hi.astype(jnp.float32)).astype(jnp.bfloat16)
    return (jnp.dot(hi, sel, preferred_element_type=jnp.float32)
            + jnp.dot(lo, sel, preferred_element_type=jnp.float32))


def _unpack(ref):
    return jnp.concatenate([ref[s] for s in range(O_SLABS)], axis=1), ref[O_SLABS]


def _prompt_out_kernel(o1_ref, lse1_ref, a4_ref, c16_ref, gate_ref, ag_ref, x_ref, w_ref, sel_ref, y_ref,
                       a16_scr):
    lane = lax.broadcasted_iota(jnp.int32, (1, LANES), 1)
    rows = OUT_TILE // CLASS_ORDER_OUT
    for s in range(ACC_SLABS):
        for r in range(CLASS_ORDER_OUT):
            a16_scr[s, pl.ds(r, rows, stride=CLASS_ORDER_OUT), :] = c16_ref[s, r]
    o1 = jnp.concatenate([o1_ref[s].astype(jnp.float32) for s in range(O_SLABS)], axis=1)
    parts = [(o1, lse1_ref[...])] + [_unpack(ref) for ref in (a4_ref, a16_scr)]
    m = functools.reduce(jnp.maximum, [lse for _, lse in parts])
    ws = [jnp.exp2(lse - m) for _, lse in parts]
    den = functools.reduce(lambda a, b: a + b, ws)
    o = None
    for w, (tiles, _) in zip(ws, parts):
        scale = jnp.where(lane < B_HEADS, w / den, 0.0)
        term = _expand_heads(scale, sel_ref[...]) * tiles
        o = term if o is None else o + term
    b = (o * gate_ref[...].astype(jnp.float32)).astype(jnp.bfloat16)
    y = x_ref[...] + jnp.dot(ag_ref[...], w_ref[0:A_WIDTH, :], preferred_element_type=jnp.float32)
    y_ref[...] = y + jnp.dot(b, w_ref[A_WIDTH:, :], preferred_element_type=jnp.float32)


def _head_select():
    h = jnp.arange(LANES)[:, None]
    c = jnp.arange(B_WIDTH)[None, :] // HEAD_DIM
    return (h == c).astype(jnp.bfloat16)


def _prompt_out(a1, a4, a16, zb, ag, x, w_out):
    t = x.shape[0]
    o1, lse1 = a1
    acc = pl.BlockSpec((ACC_SLABS, OUT_TILE, LANES), lambda i: (0, i, 0))
    row = lambda width: pl.BlockSpec((OUT_TILE, width), lambda i: (i, 0))
    const = lambda shape: pl.BlockSpec(shape, lambda i: (0, 0))
    return pl.pallas_call(
        _prompt_out_kernel,
        grid=(t // OUT_TILE,),
        in_specs=[pl.BlockSpec((O_SLABS, OUT_TILE, LANES), lambda i: (0, i, 0)), row(LANES), acc,
                  pl.BlockSpec((ACC_SLABS, CLASS_ORDER_OUT, OUT_TILE // CLASS_ORDER_OUT, LANES),
                               lambda i: (0, 0, i, 0)),
                  row(B_WIDTH), row(A_WIDTH), row(D_MODEL), const((D_MODEL, D_MODEL)),
                  const((LANES, B_WIDTH))],
        out_specs=row(D_MODEL),
        out_shape=jax.ShapeDtypeStruct((t, D_MODEL), jnp.float32),
        scratch_shapes=[pltpu.VMEM((ACC_SLABS, OUT_TILE, LANES), jnp.float32)],
        compiler_params=pltpu.CompilerParams(dimension_semantics=("arbitrary",), vmem_limit_bytes=VMEM_LIMIT),
        name="prompt_out",
    )(o1, lse1, a4, a16, zb, ag, x, w_out, _head_select())


def _transpose_by_identity(eye, t):
    out = None
    rem = t
    for _ in range(3):
        piece = rem.astype(jnp.bfloat16)
        rem = rem - piece.astype(jnp.float32)
        term = lax.dot_general(eye, piece, (((1,), (1,)), ((), ())), preferred_element_type=jnp.float32)
        out = term if out is None else out + term
    return out


def _sample_proj_kernel(x_ref, ng_ref, w_ref, sg_ref, w0_ref, b0_ref, qg_ref, kg_ref, bd_ref, eye_ref,
                        qt_ref, kt_ref, vt_ref, k_ref, v_ref, zb_ref, ag_ref, vn_ref):
    ua, vn, za, qn, kn, v, zb = _project(x_ref[...], ng_ref[...], w_ref, sg_ref[...], qg_ref[...],
                                         kg_ref[...], bd_ref[...], HEAD_DIM ** -0.5)
    mixed = w0_ref[...] * vn + b0_ref[...]
    ag_ref[...] = (ua * mixed * _silu(za)).astype(ag_ref.dtype)
    qt_ref[...] = _transpose_by_identity(eye_ref[...], qn)
    kt_ref[...] = _transpose_by_identity(eye_ref[...], kn)
    vt_ref[...] = _transpose_by_identity(eye_ref[...], v)
    k_ref[...] = kn
    v_ref[...] = v
    zb_ref[...] = zb
    vn_ref[...] = vn


def _sample_proj(x, ng, w_in, sg, w0, b0, qg, kg, bd):
    n = x.shape[0]
    f32 = jnp.float32
    wide = jax.ShapeDtypeStruct((n, B_WIDTH), f32)
    tall = jax.ShapeDtypeStruct((B_WIDTH, n), f32)
    return pl.pallas_call(
        _sample_proj_kernel,
        out_shape=[tall, tall, tall, wide, wide, wide, jax.ShapeDtypeStruct((n, A_WIDTH), jnp.bfloat16),
                   jax.ShapeDtypeStruct((n, A_WIDTH), f32)],
        compiler_params=pltpu.CompilerParams(vmem_limit_bytes=VMEM_LIMIT),
        name="sample_proj",
    )(x, ng, w_in, sg, w0, b0, qg, kg, bd, jnp.eye(B_WIDTH, dtype=jnp.bfloat16))


def _window_bias():
    dist = WINDOW - jnp.arange(WINDOW)
    count = sum(((dist % d == 0) & (dist <= win)).astype(jnp.float32) for win, d in PATTERNS)
    pen = -(_alibi_slopes()[:, None] * dist.astype(jnp.float32)[None, :])
    bias = jnp.where((count > 0)[None, :], pen, NEG_BIG)
    return bias[:, None, :], count[None, None, :]


def _sample_attn_kernel(q_ref, kn_ref, vn_ref, kt_ref, vt_ref, bias_ref, count_ref, o_ref):
    b = pl.program_id(0)
    mine = lax.broadcasted_iota(jnp.int32, (1, 1, q_ref.shape[-1]), 2) == b
    pick = lambda ref: jnp.sum(jnp.where(mine, ref[...], 0.0), axis=-1, keepdims=True)
    q = pick(q_ref)
    k_new = pick(kn_ref)
    v_new = pick(vn_ref)
    s_new = jnp.sum(k_new * q, axis=1, keepdims=True)
    s = jnp.sum(kt_ref[...] * q, axis=1, keepdims=True) + bias_ref[...]
    m = jnp.maximum(jnp.max(s, axis=-1, keepdims=True), s_new)
    p = count_ref[...] * jnp.exp(s - m)
    p_new = float(len(PATTERNS)) * jnp.exp(s_new - m)
    den = jnp.sum(p, axis=-1, keepdims=True) + p_new
    num = jnp.sum(p * vt_ref[...], axis=-1, keepdims=True) + p_new * v_new

    @pl.when(b == 0)
    def _():
        o_ref[...] = jnp.zeros_like(o_ref)

    o_ref[...] = jnp.where(mine, num / den, o_ref[...])


def _sample_attn(q, kn, vn, ck, cv, layer):
    n = q.shape[-1]
    tok = pl.BlockSpec((B_HEADS, HEAD_DIM, n), lambda b: (0, 0, 0))
    win = pl.BlockSpec((None, None, B_HEADS, HEAD_DIM, WINDOW), lambda b: (layer, b, 0, 0, 0))
    bias, count = _window_bias()
    return pl.pallas_call(
        _sample_attn_kernel,
        grid=(n,),
        in_specs=[tok, tok, tok, win, win,
                  pl.BlockSpec((B_HEADS, 1, WINDOW), lambda b: (0, 0, 0)),
                  pl.BlockSpec((1, 1, WINDOW), lambda b: (0, 0, 0))],
        out_specs=tok,
        out_shape=jax.ShapeDtypeStruct((B_HEADS, HEAD_DIM, n), jnp.float32),
        compiler_params=pltpu.CompilerParams(dimension_semantics=("arbitrary",), vmem_limit_bytes=VMEM_LIMIT),
        name="sample_attn",
    )(q, kn, vn, ck, cv, bias, count)


def _sample_out_kernel(ot_ref, zb_ref, ag_ref, x_ref, w_ref, eye_ref, y_ref):
    o = _transpose_by_identity(eye_ref[...], ot_ref[...])
    b = (o * _silu(zb_ref[...])).astype(jnp.bfloat16)
    y = x_ref[...] + jnp.dot(ag_ref[...], w_ref[0:A_WIDTH, :], preferred_element_type=jnp.float32)
    y_ref[...] = y + jnp.dot(b, w_ref[A_WIDTH:, :], preferred_element_type=jnp.float32)


def _sample_out(ot, zb, ag, x, w_out):
    return pl.pallas_call(
        _sample_out_kernel,
        out_shape=jax.ShapeDtypeStruct(x.shape, jnp.float32),
        name="sample_out",
    )(ot, zb, ag, x, w_out, jnp.eye(x.shape[0], dtype=jnp.bfloat16))


def kernel(x_prompt, x_sample, cache_k, cache_v, norm_g, w_in, sgu_g, w_spatial, b_spatial,
           q_norm_g, k_norm_g, w_out):
    depth = norm_g.shape[0]
    bp, tp, _ = x_prompt.shape
    bs, ts, _ = x_sample.shape
    assert bp == 1 and ts == 1 and cache_k.shape[2] == WINDOW and tp % (16 * BAND * ATTN_GROUP[16]) == 0
    keep = min(WINDOW, tp)
    xp = x_prompt.reshape(tp, D_MODEL)
    xs = x_sample.reshape(bs, D_MODEL)
    bd = (jnp.arange(256)[:, None] // HEAD_DIM == jnp.arange(256)[None, :] // HEAD_DIM).astype(jnp.bfloat16)
    ck = cache_k.transpose(0, 1, 3, 4, 2)
    cv = cache_v.transpose(0, 1, 3, 4, 2)
    kp_new, vp_new, ks_new, vs_new, sgu_new = [], [], [], [], []
    for l in range(depth):
        ng = norm_g[l][None, :]
        sg = sgu_g[l][None, :]
        qg = jnp.tile(q_norm_g[l], B_HEADS)[None, :]
        kg = jnp.tile(k_norm_g[l], B_HEADS)[None, :]
        w_in_l = w_in[l].astype(jnp.bfloat16)
        w_out_l = w_out[l].astype(jnp.bfloat16)
        wsp = w_spatial[l].transpose(1, 0, 2).reshape(CHUNK, A_GROUPS * CHUNK)
        bsp = jnp.repeat(b_spatial[l].T, HEAD_DIM, axis=1)

        nat, p4, p16, kf, vf, zb, ag = _prompt_proj(xp, ng, w_in_l, sg, wsp, bsp, qg, kg, bd, keep)
        a1 = _attn_pass(nat[None], 1)
        a4 = _attn_pass(p4, 4)
        a16 = _attn_pass(p16, 16)
        kp_new.append(kf.reshape(1, keep, B_HEADS, HEAD_DIM))
        vp_new.append(vf.reshape(1, keep, B_HEADS, HEAD_DIM))
        xp = _prompt_out(a1, a4, a16, zb, ag, xp, w_out_l)

        w0 = jnp.repeat(w_spatial[l, :, 0, 0], HEAD_DIM)[None, :]
        b0 = jnp.repeat(b_spatial[l, :, 0], HEAD_DIM)[None, :]
        qt, kt, vt, k, v, zbs, ags, vn = _sample_proj(xs, ng, w_in_l, sg, w0, b0, qg, kg, bd)
        heads = lambda a: a.reshape(B_HEADS, HEAD_DIM, bs)
        ot = _sample_attn(heads(qt), heads(kt), heads(vt), ck, cv, l)
        ks_new.append(k.reshape(bs, 1, B_HEADS, HEAD_DIM))
        vs_new.append(v.reshape(bs, 1, B_HEADS, HEAD_DIM))
        sgu_new.append(vn.reshape(bs, 1, A_WIDTH))
        xs = _sample_out(ot.reshape(B_WIDTH, bs), zbs, ags, xs, w_out_l)

    return (xp.reshape(bp, tp, D_MODEL), xs.reshape(bs, ts, D_MODEL),
            jnp.stack(kp_new, axis=0), jnp.stack(vp_new, axis=0),
            jnp.stack(ks_new, axis=0), jnp.stack(vs_new, axis=0), jnp.stack(sgu_new, axis=0))
```
